```python
import math
import jax
import jax.numpy as jnp
from jax import lax
import numpy as np

D_MODEL = 1024
BATCH = 8
SEQ = 8192
DEPTH = 2

N_A = DEPTH // 2
N_B = DEPTH - N_A
PLE_DIM = 256
RWKV_HEAD = 64
RWKV_HEADS = D_MODEL // RWKV_HEAD
DECAY_LORA = 64
ICLR_LORA = 64
GN_EPS = 64e-5
DIFF_HEAD = 64
DIFF_HEADS = D_MODEL // (2 * DIFF_HEAD)
ROPE_DIMS = DIFF_HEAD // 4
ROPE_THETA = 500000.0
Q_BLOCK = 128
NORM_EPS = 1e-6
SUBLN_EPS = 1e-5

kernel_name = 'yoco_rwkv7_diffattn_sandwich_ple'


def rms_norm(x, g, eps=NORM_EPS):
    xf = x.astype(jnp.float32)
    y = xf * lax.rsqrt(jnp.mean(xf * xf, axis=-1, keepdims=True) + eps)
    return (y * g.astype(jnp.float32)).astype(x.dtype)


def rwkv7_time_mix(xn, mu, w_in, w0, w1, w2, a0, a1, a2, k_k, k_a, r_k, lnx_g, lnx_b, w_out):
    B, S, D = xn.shape
    H, N = RWKV_HEADS, RWKV_HEAD
    f32 = jnp.float32
    x_prev = jnp.pad(xn, ((0, 0), (1, 0), (0, 0)))[:, :-1]
    xmix = xn[None] + (x_prev - xn)[None] * mu[:, None, None, :]
    rkvg = jnp.einsum('cbsd,cde->cbse', xmix[:4], w_in)
    r, k, v, g = rkvg[0], rkvg[1], rkvg[2], rkvg[3]
    xw, xa = xmix[4], xmix[5]
    w_pre = (w0 + jnp.tanh(xw @ w1) @ w2).astype(f32)
    decay = jnp.exp(-jnp.exp(-jax.nn.softplus(-w_pre) - 0.5))
    a = jax.nn.sigmoid((a0 + (xa @ a1) @ a2).astype(f32))
    heads = lambda t: t.astype(f32).reshape(B, S, H, N)
    r, k, v, decay, a = heads(r), heads(k), heads(v), heads(decay), heads(a)
    kk = k * k_k.astype(f32).reshape(H, N)
    kk = kk / jnp.maximum(jnp.linalg.norm(kk, axis=-1, keepdims=True), 1e-12)
    k = k * (1.0 + (a - 1.0) * k_a.astype(f32).reshape(H, N))

    def step(state, inp):
        r_t, w_t, k_t, v_t, kk_t, a_t = inp
        s_kk = jnp.einsum('bhij,bhj->bhi', state, kk_t)
        state = (state * w_t[:, :, None, :]
                 - s_kk[..., None] * (kk_t * a_t)[:, :, None, :]
                 + v_t[..., None] * k_t[:, :, None, :])
        return state, jnp.einsum('bhij,bhj->bhi', state, r_t)

    seq_first = lambda t: jnp.swapaxes(t, 0, 1)
    state0 = jnp.zeros((B, H, N, N), f32)
    _, y = lax.scan(step, state0, (seq_first(r), seq_first(decay), seq_first(k),
                                   seq_first(v), seq_first(kk), seq_first(a)))
    y = jnp.swapaxes(y, 0, 1)
    mean = jnp.mean(y, axis=-1, keepdims=True)
    var = jnp.mean(jnp.square(y - mean), axis=-1, keepdims=True)
    y = ((y - mean) * lax.rsqrt(var + GN_EPS) * lnx_g.astype(f32).reshape(H, N)
         + lnx_b.astype(f32).reshape(H, N))
    y = y + jnp.sum(r * k * r_k.astype(f32), axis=-1, keepdims=True) * v
    y = y.reshape(B, S, D).astype(xn.dtype) * jax.nn.silu(g)
    return y @ w_out


def rope_tables(S):
    inv = ROPE_THETA ** (-jnp.arange(0, ROPE_DIMS, 2, dtype=jnp.float32) / ROPE_DIMS)
    ang = jnp.arange(S, dtype=jnp.float32)[:, None] * inv[None, :]
    return jnp.cos(ang), jnp.sin(ang)


def partial_rope(x, cos, sin):
    half = ROPE_DIMS // 2
    c = cos[None, :, None, None, :].astype(x.dtype)
    s = sin[None, :, None, None, :].astype(x.dtype)
    x1, x2 = x[..., :half], x[..., half:ROPE_DIMS]
    return jnp.concatenate([x1 * c - x2 * s, x2 * c + x1 * s, x[..., ROPE_DIMS:]], axis=-1)


def shared_kv(h, kv_norm, kv_w, cos, sin):
    B, S, D = h.shape
    kv = rms_norm(h, kv_norm) @ kv_w
    k = partial_rope(kv[..., :D].reshape(B, S, DIFF_HEADS, 2, DIFF_HEAD), cos, sin)
    v = kv[..., D:].reshape(B, S, DIFF_HEADS, 2 * DIFF_HEAD)
    return k.transpose(0, 2, 3, 1, 4), v.transpose(0, 2, 1, 3)


def diff_attention_mix(xn, k_sh, v_sh, w_in, lam_qk, subln_g, w_out, lam_init, cos, sin):
    B, S, D = xn.shape
    H, Dh = DIFF_HEADS, DIFF_HEAD
    f32 = jnp.float32
    proj = xn @ w_in
    q = partial_rope(proj[..., :D].reshape(B, S, H, 2, Dh), cos, sin) * (Dh ** -0.5)
    gate = proj[..., D:]
    lq = lam_qk.astype(f32)
    lam = jnp.exp(jnp.sum(lq[0] * lq[1])) - jnp.exp(jnp.sum(lq[2] * lq[3])) + lam_init
    nblk = S // Q_BLOCK
    qb = q.reshape(B, nblk, Q_BLOCK, H, 2, Dh).transpose(1, 0, 3, 4, 2, 5)
    kf = k_sh.astype(f32)
    vf = v_sh.astype(f32)
    k_pos = jnp.arange(S)

    def block(args):
        q_blk, blk = args
        s = jnp.einsum('bhcqd,bhckd->bhcqk', q_blk.astype(f32), kf)
        q_pos = blk * Q_BLOCK + jnp.arange(Q_BLOCK)
        s = jnp.where(k_pos[None, :] <= q_pos[:, None], s, -1e30)
        prob = jax.nn.softmax(s, axis=-1)
        attn = prob[:, :, 0] - lam * prob[:, :, 1]
        return jnp.einsum('bhqk,bhkv->bhqv', attn, vf)

    o = lax.map(block, (qb, jnp.arange(nblk)))
    o = o.transpose(1, 0, 3, 2, 4).reshape(B, S, H, 2 * Dh)
    o = rms_norm(o, subln_g, SUBLN_EPS) * (1.0 - lam_init)
    o = o.reshape(B, S, D).astype(xn.dtype) * jax.nn.silu(gate)
    return o @ w_out


def setup_inputs(seed: int = 0) -> dict:
    key = jax.random.key(seed)
    ks = iter(jax.random.split(key, 32))
    D, f32 = D_MODEL, jnp.float32
    nrm = lambda shape, scale: jax.random.normal(next(ks), shape, f32) * scale
    gain = lambda shape: 1.0 + nrm(shape, 0.02)
    return {
        'x': nrm((BATCH, SEQ, D), 1.0),
        'p': nrm((DEPTH, BATCH, SEQ, PLE_DIM), 1.0),
        'norm_pre': gain((DEPTH, D)),
        'norm_post': gain((DEPTH, D)),
        'a_mu': jax.random.uniform(next(ks), (N_A, 6, D), f32),
        'a_w_in': nrm((N_A, 4, D, D), D ** -0.5),
        'a_w0': jax.random.uniform(next(ks), (N_A, D), f32, -6.0, 1.0),
        'a_w1': nrm((N_A, D, DECAY_LORA), D ** -0.5),
        'a_w2': nrm((N_A, DECAY_LORA, D), 0.1),
        'a_a0': nrm((N_A, D), 0.5),
        'a_a1': nrm((N_A, D, ICLR_LORA), D ** -0.5),
        'a_a2': nrm((N_A, ICLR_LORA, D), 0.5 * ICLR_LORA ** -0.5),
        'a_k_k': 1.0 + nrm((N_A, D), 0.1),
        'a_k_a': 1.0 + nrm((N_A, D), 0.1),
        'a_r_k': nrm((N_A, RWKV_HEADS, RWKV_HEAD), 0.1),
        'a_lnx_g': gain((N_A, D)),
        'a_lnx_b': nrm((N_A, D), 0.02),
        'a_w_out': nrm((N_A, D, D), D ** -0.5),
        'kv_norm': gain((D,)),
        'kv_w': nrm((D, 2 * D), D ** -0.5),
        'b_w_in': nrm((N_B, D, 2 * D), D ** -0.5),
        'b_lambda': nrm((N_B, 4, DIFF_HEAD), 0.1),
        'b_subln': gain((N_B, 2 * DIFF_HEAD)),
        'b_w_out': nrm((N_B, D, D), D ** -0.5),
        'ple_w': nrm((DEPTH, PLE_DIM, D), PLE_DIM ** -0.5),
        'ple_gate': nrm((DEPTH, D, D), D ** -0.5),
        'ple_norm': gain((DEPTH, D)),
    }


def reference(x, p, norm_pre, norm_post, a_mu, a_w_in, a_w0, a_w1, a_w2, a_a0, a_a1, a_a2,
              a_k_k, a_k_a, a_r_k, a_lnx_g, a_lnx_b, a_w_out, kv_norm, kv_w, b_w_in, b_lambda,
              b_subln, b_w_out, ple_w, ple_gate, ple_norm):
    S = x.shape[1]
    cos, sin = rope_tables(S)
    h = x
    k_sh = None
    v_sh = None
    for i in range(DEPTH):
        xn = rms_norm(h, norm_pre[i])
        if i < N_A:
            y = rwkv7_time_mix(xn, a_mu[i], a_w_in[i], a_w0[i], a_w1[i], a_w2[i], a_a0[i],
                               a_a1[i], a_a2[i], a_k_k[i], a_k_a[i], a_r_k[i], a_lnx_g[i],
                               a_lnx_b[i], a_w_out[i])
        else:
            if i == N_A:
                k_sh, v_sh = shared_kv(h, kv_norm, kv_w, cos, sin)
            j = i - N_A
            lam_init = 0.8 - 0.6 * math.exp(-0.3 * i)
            y = diff_attention_mix(xn, k_sh, v_sh, b_w_in[j], b_lambda[j], b_subln[j],
                                   b_w_out[j], lam_init, cos, sin)
        h = h + rms_norm(y, norm_post[i])
        e = p[i] @ ple_w[i]
        g = jax.nn.sigmoid(h @ ple_gate[i])
        h = h + rms_norm(g * e, ple_norm[i])
    return h
```

```python
import functools
import math

import jax
import jax.numpy as jnp
from jax import lax
from jax.experimental import pallas as pl
from jax.experimental.pallas import tpu as pltpu

F32 = jnp.float32
BF16 = jnp.bfloat16

HEAD = 64
PAIR = 2 * HEAD
CHUNK = 64
ROPE_DIMS = 16
ROPE_THETA = 500000.0
NORM_EPS = 1e-6
SUBLN_EPS = 1e-5
GN_EPS = 64e-5
LAM_INIT_1 = 0.8 - 0.6 * math.exp(-0.3 * 1)
DECAY_SCALE = math.exp(-0.5)
VMEM_LIMIT = 56 * 1024 * 1024


def _dot(a, b):
    return jnp.dot(a.astype(BF16), b.astype(BF16), preferred_element_type=F32)


def _dot_nt(a, b):
    return lax.dot_general(a.astype(BF16), b.astype(BF16), (((1,), (1,)), ((), ())),
                           preferred_element_type=F32)


def _dot_tn(a, b):
    return lax.dot_general(a.astype(BF16), b.astype(BF16), (((0,), (0,)), ((), ())),
                           preferred_element_type=F32)


def _split(a):
    hi = a.astype(BF16)
    lo = (a - hi.astype(F32)).astype(BF16)
    return hi, lo


def _dot_lhs_split(a, b):
    hi, lo = _split(a)
    return _dot(hi, b) + _dot(lo, b)


def _dot_rhs_split(a, b):
    hi, lo = _split(b)
    return _dot(a, hi) + _dot(a, lo)


def _dot_split(a, b):
    ah, al = _split(a)
    bh, bl = _split(b)
    return _dot(ah, bh) + (_dot(ah, bl) + _dot(al, bh))


def _rms(x, g, eps):
    return x * lax.rsqrt(jnp.mean(x * x, axis=-1, keepdims=True) + eps) * g


def _sigmoid(x):
    return 1.0 / (1.0 + jnp.exp(-x))


def _rope(x, c, s1, s2):
    d = x.shape[-1]
    rep = d // PAIR
    c = jnp.tile(c, (1, rep))
    s1 = jnp.tile(s1, (1, rep))
    s2 = jnp.tile(s2, (1, rep))
    up = pltpu.roll(x, d - ROPE_DIMS // 2, 1)
    dn = pltpu.roll(x, ROPE_DIMS // 2, 1)
    return x * c + up * s1 + dn * s2


def _const_spec(shape):
    nd = len(shape)
    return pl.BlockSpec(shape, lambda *_: (0,) * nd)


def _rwkv_in_kernel(x_ref, xp_ref, g_ref, mu_ref, win_ref, w0_ref, w1_ref, w2_ref, a0_ref, a1_ref,
                    a2_ref, kk_ref, ka_ref, rk_ref, seg_ref, segt_ref,
                    r_out, k_out, v_out, g_out, kkn_out, beta_out, logw_out, bonus_out):
    s = pl.program_id(1)
    gain = g_ref[...]
    xn = _rms(x_ref[0], gain, NORM_EPS)
    t = xn.shape[0]
    prev_last = _rms(xp_ref[0], gain, NORM_EPS)[7:8]
    prev_last = jnp.where(s > 0, prev_last, 0.0)
    row = lax.broadcasted_iota(jnp.int32, (t, 1), 0)
    xprev = jnp.where(row == 0, prev_last, pltpu.roll(xn, 1, 0))
    dx = xprev - xn
    mix = lambda c: xn + dx * mu_ref[c:c + 1, :]

    r = _dot(mix(0), win_ref[0])
    k = _dot(mix(1), win_ref[1])
    v = _dot(mix(2), win_ref[2])
    g = _dot(mix(3), win_ref[3])
    w_pre = w0_ref[...] + _dot(jnp.tanh(_dot(mix(4), w1_ref[...])), w2_ref[...])
    logw = -DECAY_SCALE * _sigmoid(w_pre)
    a = _sigmoid(a0_ref[...] + _dot(_dot(mix(5), a1_ref[...]), a2_ref[...]))

    seg = seg_ref[...]
    segt = segt_ref[...]
    kk = k * kk_ref[...]
    sumsq = _dot_lhs_split(_dot(kk * kk, seg), segt)
    kk = kk / jnp.maximum(jnp.sqrt(sumsq), 1e-12)
    k = k * (1.0 + (a - 1.0) * ka_ref[...])
    rk = _dot_lhs_split(_dot(r * k * rk_ref[...], seg), segt)

    r_out[0] = r.astype(BF16)
    k_out[0] = k.astype(BF16)
    v_out[0] = v.astype(BF16)
    g_out[0] = g.astype(BF16)
    kkn_out[0] = kk.astype(BF16)
    beta_out[0] = (kk * a).astype(BF16)
    logw_out[0] = logw
    bonus_out[0] = (rk * v).astype(BF16)


def _rwkv_in(x, norm_pre, mu, w_in, w0, w1, w2, a0, a1, a2, k_k, k_a, r_k, seg, segt, tile):
    b, s, d = x.shape
    lora = w1.shape[1]
    tok = pl.BlockSpec((1, tile, d), lambda i, j: (i, j, 0))
    prev = pl.BlockSpec((1, 8, d), lambda i, j: (i, jnp.maximum(j * (tile // 8) - 1, 0), 0))
    row = _const_spec((1, d))
    out_bf = jax.ShapeDtypeStruct((b, s, d), BF16)
    out_f32 = jax.ShapeDtypeStruct((b, s, d), F32)
    return pl.pallas_call(
        _rwkv_in_kernel,
        grid=(b, s // tile),
        in_specs=[tok, prev, row, _const_spec((6, d)), _const_spec((4, d, d)), row,
                  _const_spec((d, lora)), _const_spec((lora, d)), row, _const_spec((d, lora)),
                  _const_spec((lora, d)), row, row, row, _const_spec((d, PAIR)), _const_spec((PAIR, d))],
        out_specs=[tok] * 8,
        out_shape=[out_bf] * 6 + [out_f32, out_bf],
        compiler_params=pltpu.CompilerParams(dimension_semantics=("parallel", "parallel"),
                                             vmem_limit_bytes=VMEM_LIMIT),
        name="rwkv_in",
    )(x, x, norm_pre, mu, w_in, w0, w1, w2, a0, a1, a2, k_k, k_a, r_k, seg, segt)


def _unit_lower_inverse(a):
    n = a.shape[0]
    i = lax.broadcasted_iota(jnp.int32, (n, n), 0)
    j = lax.broadcasted_iota(jnp.int32, (n, n), 1)
    inv = jnp.where(i == j, 1.0, 0.0) + jnp.where((i // 2 == j // 2) & (i > j), a, 0.0)
    half = 2
    while half < CHUNK:
        blk = 2 * half
        sel = (i // blk == j // blk) & (i % blk >= half) & (j % blk < half)
        inv = inv + _dot_split(_dot_split(inv, jnp.where(sel, a, 0.0)), inv)
        half = blk
    return inv


def _rwkv_scan_kernel(r_ref, k_ref, v_ref, kk_ref, beta_ref, logw_ref, y_ref, z_ref):
    @pl.when(pl.program_id(2) == 0)
    def _():
        z_ref[...] = jnp.zeros_like(z_ref)

    t = r_ref.shape[1]
    c = CHUNK
    ri = lax.broadcasted_iota(jnp.int32, (PAIR, PAIR), 0)
    ci = lax.broadcasted_iota(jnp.int32, (PAIR, PAIR), 1)
    same_head = (ri // HEAD) == (ci // HEAD)
    strict_bd = same_head & (ri > ci)
    eye = ri == ci
    ti = lax.broadcasted_iota(jnp.int32, (c, PAIR), 0)
    si = lax.broadcasted_iota(jnp.int32, (c, PAIR), 1)
    incl = (si % HEAD) <= ti
    head0 = si < HEAD
    tri = jnp.where(lax.broadcasted_iota(jnp.int32, (c, c), 1)
                    <= lax.broadcasted_iota(jnp.int32, (c, c), 0), 1.0, 0.0).astype(BF16)

    def stack_heads(m):
        return jnp.concatenate([jnp.where(head0, m, 0.0), jnp.where(head0, 0.0, m)], axis=0)

    z = z_ref[...]
    for ch in range(t // c):
        rows = pl.ds(ch * c, c)
        r = r_ref[0, rows, :].astype(F32)
        k = k_ref[0, rows, :].astype(F32)
        v = v_ref[0, rows, :].astype(F32)
        kk = kk_ref[0, rows, :].astype(F32)
        beta = beta_ref[0, rows, :].astype(F32)
        lw = logw_ref[0, rows, :]

        cum = _dot_rhs_split(tri, lw)
        cum_end = cum[c - 1:c, :]
        e_neg = jnp.exp(-cum)
        e_end = jnp.exp(cum_end - cum)
        r_t = r * jnp.exp(cum)
        a_t = -kk * jnp.exp(cum - lw)
        k_h = k * e_neg
        b_h = beta * e_neg
        k_e = k * e_end
        b_e = beta * e_end

        xa = stack_heads(a_t)
        vst = stack_heads(v)
        a_ab = jnp.where(strict_bd, _dot_nt(xa, jnp.concatenate([b_h, b_h], axis=0)), 0.0)
        a_ak = jnp.where(strict_bd, _dot_nt(xa, jnp.concatenate([k_h, k_h], axis=0)), 0.0)
        tinv = _unit_lower_inverse(a_ab)
        wm_st = _dot_split(tinv, xa)
        um_st = _dot_split(tinv, _dot(a_ak, vst))
        a_qb = jnp.where(incl, _dot_nt(r_t, stack_heads(b_h)), 0.0)
        a_qk = jnp.where(incl, _dot_nt(r_t, stack_heads(k_h)), 0.0)
        rq = r_t + _dot(a_qb, wm_st)
        y_in = _dot(a_qb, um_st) + _dot(a_qk, vst)
        wm = wm_st[:c] + wm_st[c:]
        um = um_st[:c] + um_st[c:]
        p_t = (jnp.where(eye, jnp.exp(cum_end), 0.0)
               + jnp.where(same_head, _dot_tn(b_e, wm), 0.0))
        q_t = jnp.where(same_head, _dot_tn(b_e, um) + _dot_tn(k_e, v), 0.0)

        y_ref[0, rows, :] = (_dot(rq, z) + y_in).astype(y_ref.dtype)
        z = _dot_split(p_t, z) + q_t
    z_ref[...] = z


def _rwkv_scan(r, k, v, kk, beta, logw, tile):
    b, s, d = r.shape
    tok = pl.BlockSpec((1, tile, PAIR), lambda i, h, j: (i, j, h))
    return pl.pallas_call(
        _rwkv_scan_kernel,
        grid=(b, d // PAIR, s // tile),
        in_specs=[tok] * 6,
        out_specs=tok,
        out_shape=jax.ShapeDtypeStruct((b, s, d), BF16),
        scratch_shapes=[pltpu.VMEM((PAIR, PAIR), F32)],
        compiler_params=pltpu.CompilerParams(
            dimension_semantics=("parallel", "parallel", "arbitrary"), vmem_limit_bytes=VMEM_LIMIT),
        name="rwkv_scan",
    )(r, k, v, kk, beta, logw)


def _ple(h, p, ple_w, ple_gate, ple_norm):
    e = _dot(p, ple_w)
    gate = _sigmoid(_dot(h, ple_gate))
    return h + _rms(gate * e, ple_norm, NORM_EPS)


def _mid_kernel(y_ref, g_ref, bonus_ref, x_ref, p_ref, seg_ref, segt_ref, lng_ref, lnb_ref, wout_ref,
                npost_ref, plew_ref, pleg_ref, plen_ref, kvn_ref, kvw_ref, npre1_ref, bwin_ref,
                cos_ref, s1_ref, s2_ref,
                h_out, q_out, k_out, v_out, gate_out):
    d = x_ref.shape[-1]
    seg = seg_ref[...]
    segt = segt_ref[...]
    y = y_ref[0].astype(F32)
    mean = _dot_lhs_split(_dot_lhs_split(y, seg), segt) * (1.0 / HEAD)
    yc = y - mean
    var = _dot_lhs_split(_dot(yc * yc, seg), segt) * (1.0 / HEAD)
    yn = yc * lax.rsqrt(var + GN_EPS) * lng_ref[...] + lnb_ref[...]
    yn = yn + bonus_ref[0].astype(F32)
    g = g_ref[0].astype(F32)
    mixed = _dot(yn * (g * _sigmoid(g)), wout_ref[...])
    h = x_ref[0] + _rms(mixed, npost_ref[...], NORM_EPS)
    h = _ple(h, p_ref[0, 0], plew_ref[...], pleg_ref[...], plen_ref[...])
    h_out[0] = h

    cos, s1, s2 = cos_ref[...], s1_ref[...], s2_ref[...]
    kv = _dot(_rms(h, kvn_ref[...], NORM_EPS), kvw_ref[...])
    k_out[0] = _rope(kv[:, :d], cos, s1, s2).astype(BF16)
    v_out[0] = kv[:, d:].astype(BF16)
    proj = _dot(_rms(h, npre1_ref[...], NORM_EPS), bwin_ref[...])
    q_out[0] = (_rope(proj[:, :d], cos, s1, s2) * (HEAD ** -0.5)).astype(BF16)
    gate_out[0] = proj[:, d:].astype(BF16)


def _mid(y, g, bonus, x, p, seg, segt, lnx_g, lnx_b, w_out, norm_post, ple_w, ple_gate, ple_norm,
         kv_norm, kv_w, norm_pre1, b_w_in, cos, s1, s2, tile):
    b, s, d = x.shape
    pd = p.shape[-1]
    tok = pl.BlockSpec((1, tile, d), lambda i, j: (i, j, 0))
    ptok = pl.BlockSpec((1, 1, tile, pd), lambda i, j: (0, i, j, 0))
    tab = pl.BlockSpec((tile, PAIR), lambda i, j: (j, 0))
    row = _const_spec((1, d))
    out_bf = jax.ShapeDtypeStruct((b, s, d), BF16)
    return pl.pallas_call(
        _mid_kernel,
        grid=(b, s // tile),
        in_specs=[tok, tok, tok, tok, ptok, _const_spec((d, PAIR)), _const_spec((PAIR, d)), row, row,
                  _const_spec((d, d)), row, _const_spec((pd, d)), _const_spec((d, d)), row, row,
                  _const_spec((d, 2 * d)), row, _const_spec((d, 2 * d)), tab, tab, tab],
        out_specs=[tok] * 5,
        out_shape=[jax.ShapeDtypeStruct((b, s, d), F32)] + [out_bf] * 4,
        compiler_params=pltpu.CompilerParams(dimension_semantics=("parallel", "parallel"),
                                             vmem_limit_bytes=VMEM_LIMIT),
        name="mid",
    )(y, g, bonus, x, p, seg, segt, lnx_g, lnx_b, w_out, norm_post, ple_w, ple_gate, ple_norm,
      kv_norm, kv_w, norm_pre1, b_w_in, cos, s1, s2)


def _diff_attn_kernel(q_ref, k_ref, v_ref, lam_ref, subln_ref, o_ref, qs_ref, m_ref, l_ref, acc_ref):
    i = pl.program_id(2)
    tq = q_ref.shape[1]
    q = q_ref[0]
    lane = lax.broadcasted_iota(jnp.int32, q.shape, 1)
    zero = jnp.zeros_like(q)
    qs_ref[...] = jnp.concatenate([jnp.where(lane < HEAD, q, zero), jnp.where(lane < HEAD, zero, q)], axis=0)
    m_ref[...] = jnp.full_like(m_ref, -1e30)
    l_ref[...] = jnp.zeros_like(l_ref)
    acc_ref[...] = jnp.zeros_like(acc_ref)

    def step(j, masked):
        keys = pl.ds(pl.multiple_of(j * tq, tq), tq)
        s = _dot_nt(qs_ref[...], k_ref[0, keys, :])
        if masked:
            qpos = lax.broadcasted_iota(jnp.int32, s.shape, 0) % tq
            kpos = lax.broadcasted_iota(jnp.int32, s.shape, 1)
            s = jnp.where(kpos <= qpos, s, -1e30)
        m_old = m_ref[...]
        m_new = jnp.maximum(m_old, jnp.max(s, axis=-1, keepdims=True))
        p = jnp.exp(s - m_new)
        scale = jnp.exp(m_old - m_new)
        l_ref[...] = scale * l_ref[...] + jnp.sum(p, axis=-1, keepdims=True)
        acc_ref[...] = scale * acc_ref[...] + _dot(p, v_ref[0, keys, :])
        m_ref[...] = m_new

    def body(j, carry):
        step(j, False)
        return carry

    lax.fori_loop(0, i, body, 0)
    step(i, True)

    lq = lam_ref[...]
    lam = (jnp.exp(jnp.sum(lq[0:1] * lq[1:2], axis=-1, keepdims=True))
           - jnp.exp(jnp.sum(lq[2:3] * lq[3:4], axis=-1, keepdims=True)) + LAM_INIT_1)
    o = acc_ref[...] / l_ref[...]
    o = o[:tq] - lam * o[tq:]
    o = _rms(o, subln_ref[...], SUBLN_EPS) * (1.0 - LAM_INIT_1)
    o_ref[0] = o.astype(o_ref.dtype)


def _diff_attn(q, k, v, lam_qk, subln, tile):
    b, s, d = q.shape
    qtok = pl.BlockSpec((1, tile, PAIR), lambda i, h, j: (i, j, h))
    seq = pl.BlockSpec((1, s, PAIR), lambda i, h, j: (i, 0, h))
    return pl.pallas_call(
        _diff_attn_kernel,
        grid=(b, d // PAIR, s // tile),
        in_specs=[qtok, seq, seq, _const_spec(lam_qk.shape), _const_spec((1, PAIR))],
        out_specs=qtok,
        out_shape=jax.ShapeDtypeStruct((b, s, d), BF16),
        scratch_shapes=[pltpu.VMEM((2 * tile, PAIR), BF16), pltpu.VMEM((2 * tile, 1), F32),
                        pltpu.VMEM((2 * tile, 1), F32), pltpu.VMEM((2 * tile, PAIR), F32)],
        compiler_params=pltpu.CompilerParams(
            dimension_semantics=("parallel", "parallel", "arbitrary"), vmem_limit_bytes=VMEM_LIMIT),
        name="diff_attn",
    )(q, k, v, lam_qk, subln)


def _attn_out_kernel(o_ref, gate_ref, h_ref, p_ref, wout_ref, npost_ref, plew_ref, pleg_ref, plen_ref,
                     out_ref):
    g = gate_ref[0].astype(F32)
    mixed = _dot(o_ref[0].astype(F32) * (g * _sigmoid(g)), wout_ref[...])
    h = h_ref[0] + _rms(mixed, npost_ref[...], NORM_EPS)
    out_ref[0] = _ple(h, p_ref[0, 0], plew_ref[...], pleg_ref[...], plen_ref[...])


def _attn_out(o, gate, h, p, w_out, norm_post, ple_w, ple_gate, ple_norm, tile):
    b, s, d = h.shape
    pd = p.shape[-1]
    tok = pl.BlockSpec((1, tile, d), lambda i, j: (i, j, 0))
    ptok = pl.BlockSpec((1, 1, tile, pd), lambda i, j: (1, i, j, 0))
    row = _const_spec((1, d))
    return pl.pallas_call(
        _attn_out_kernel,
        grid=(b, s // tile),
        in_specs=[tok, tok, tok, ptok, _const_spec((d, d)), row, _const_spec((pd, d)),
                  _const_spec((d, d)), row],
        out_specs=tok,
        out_shape=jax.ShapeDtypeStruct((b, s, d), F32),
        compiler_params=pltpu.CompilerParams(dimension_semantics=("parallel", "parallel"),
                                             vmem_limit_bytes=VMEM_LIMIT),
        name="attn_out",
    )(o, gate, h, p, w_out, norm_post, ple_w, ple_gate, ple_norm)


def _rope_patterns(s):
    half = ROPE_DIMS // 2
    inv = ROPE_THETA ** (-jnp.arange(0, ROPE_DIMS, 2, dtype=F32) / ROPE_DIMS)
    ang = jnp.arange(s, dtype=F32)[:, None] * inv[None, :]
    cos, sin = jnp.cos(ang), jnp.sin(ang)
    lane = jnp.arange(PAIR) % HEAD
    idx = lane % half
    c = jnp.where(lane < ROPE_DIMS, cos[:, idx], 1.0)
    s1 = jnp.where(lane < half, -sin[:, idx], 0.0)
    s2 = jnp.where((lane >= half) & (lane < ROPE_DIMS), sin[:, idx], 0.0)
    return c, s1, s2


def kernel(x, p, norm_pre, norm_post, a_mu, a_w_in, a_w0, a_w1, a_w2, a_a0, a_a1, a_a2, a_k_k, a_k_a,
           a_r_k, a_lnx_g, a_lnx_b, a_w_out, kv_norm, kv_w, b_w_in, b_lambda, b_subln, b_w_out, ple_w,
           ple_gate, ple_norm):
    b, s, d = x.shape
    assert d % PAIR == 0 and s % CHUNK == 0
    tile = min(256, s)
    assert s % tile == 0 and tile % CHUNK == 0
    row = lambda t: t.reshape(1, d).astype(F32)
    bf = lambda t: t.astype(BF16)

    head_of_lane = jnp.arange(d)[:, None] // HEAD
    seg = (head_of_lane == jnp.arange(PAIR)[None, :]).astype(BF16)
    segt = seg.T
    cos, s1, s2 = _rope_patterns(s)

    r, k, v, g, kk, beta, logw, bonus = _rwkv_in(
        x, row(norm_pre[0]), a_mu[0], bf(a_w_in[0]), row(a_w0[0]), bf(a_w1[0]), bf(a_w2[0]),
        row(a_a0[0]), bf(a_a1[0]), bf(a_a2[0]), row(a_k_k[0]), row(a_k_a[0]), row(a_r_k[0]),
        seg, segt, tile)
    y = _rwkv_scan(r, k, v, kk, beta, logw, tile)
    h, q, kr, vv, gate = _mid(
        y, g, bonus, x, p, seg, segt, row(a_lnx_g[0]), row(a_lnx_b[0]), bf(a_w_out[0]),
        row(norm_post[0]), bf(ple_w[0]), bf(ple_gate[0]), row(ple_norm[0]), row(kv_norm), bf(kv_w),
        row(norm_pre[1]), bf(b_w_in[0]), cos, s1, s2, tile)
    o = _diff_attn(q, kr, vv, b_lambda[0].astype(F32), b_subln[0].reshape(1, PAIR).astype(F32), tile)
    return _attn_out(o, gate, h, p, bf(b_w_out[0]), row(norm_post[1]), bf(ple_w[1]), bf(ple_gate[1]),
                     row(ple_norm[1]), tile)
```

```python
import math

import jax
import jax.numpy as jnp
from jax import lax
from jax.experimental import pallas as pl
from jax.experimental.pallas import tpu as pltpu

F32 = jnp.float32
BF16 = jnp.bfloat16

HEAD = 64
PAIR = 2 * HEAD
CHUNK = 64
ROPE_DIMS = 16
ROPE_THETA = 500000.0
NORM_EPS = 1e-6
SUBLN_EPS = 1e-5
GN_EPS = 64e-5
LAM_INIT_1 = 0.8 - 0.6 * math.exp(-0.3 * 1)
DECAY_SCALE = math.exp(-0.5)
LOG2E = math.log2(math.e)
MASK_VALUE = -1e30
VMEM_LIMIT = 56 * 1024 * 1024


def _tiles(s):
    proj = min(256, s)
    scan = min(2 * CHUNK, s)
    kv = min(1024, s)
    q = min(512, s)
    rows = min(512, q)
    assert s % proj == 0 and s % scan == 0 and s % kv == 0 and kv % q == 0 and scan % CHUNK == 0
    return proj, scan, q, kv, rows


def _dot(a, b):
    return jnp.dot(a.astype(BF16), b.astype(BF16), preferred_element_type=F32)


def _dot_nt(a, b):
    return lax.dot_general(a.astype(BF16), b.astype(BF16), (((1,), (1,)), ((), ())),
                           preferred_element_type=F32)


def _dot_tn(a, b):
    return lax.dot_general(a.astype(BF16), b.astype(BF16), (((0,), (0,)), ((), ())),
                           preferred_element_type=F32)


def _split(a):
    hi = a.astype(BF16)
    lo = (a - hi.astype(F32)).astype(BF16)
    return hi, lo


def _dot_lhs_split(a, b):
    hi, lo = _split(a)
    return _dot(hi, b) + _dot(lo, b)


def _rms(x, g, eps):
    return x * lax.rsqrt(jnp.mean(x * x, axis=-1, keepdims=True) + eps) * g


def _sigmoid(x):
    return 1.0 / (1.0 + jnp.exp(-x))


def _rope(x, c, s1, s2):
    d = x.shape[-1]
    rep = d // PAIR
    c = jnp.tile(c, (1, rep))
    s1 = jnp.tile(s1, (1, rep))
    s2 = jnp.tile(s2, (1, rep))
    up = pltpu.roll(x, d - ROPE_DIMS // 2, 1)
    dn = pltpu.roll(x, ROPE_DIMS // 2, 1)
    return x * c + up * s1 + dn * s2


def _const_spec(shape):
    nd = len(shape)
    return pl.BlockSpec(shape, lambda *_: (0,) * nd)


def _rwkv_in_kernel(x_ref, xp_ref, g_ref, mu_ref, win_ref, w0_ref, w1_ref, w2_ref, a0_ref, a1_ref,
                    a2_ref, kk_ref, ka_ref, rk_ref, seg_ref, segt_ref,
                    r_out, k_out, v_out, g_out, kkn_out, beta_out, logw_out, bonus_out):
    s = pl.program_id(1)
    gain = g_ref[...]
    xn = _rms(x_ref[0], gain, NORM_EPS)
    t = xn.shape[0]
    prev_last = _rms(xp_ref[0], gain, NORM_EPS)[7:8]
    prev_last = jnp.where(s > 0, prev_last, 0.0)
    row = lax.broadcasted_iota(jnp.int32, (t, 1), 0)
    xprev = jnp.where(row == 0, prev_last, pltpu.roll(xn, 1, 0))
    dx = xprev - xn
    mix = lambda c: xn + dx * mu_ref[c:c + 1, :]

    r = _dot(mix(0), win_ref[0])
    k = _dot(mix(1), win_ref[1])
    v = _dot(mix(2), win_ref[2])
    g = _dot(mix(3), win_ref[3])
    w_pre = w0_ref[...] + _dot(jnp.tanh(_dot(mix(4), w1_ref[...])), w2_ref[...])
    logw = -DECAY_SCALE * _sigmoid(w_pre)
    a = _sigmoid(a0_ref[...] + _dot(_dot(mix(5), a1_ref[...]), a2_ref[...]))

    seg = seg_ref[...]
    segt = segt_ref[...]
    kk = k * kk_ref[...]
    sumsq = _dot_lhs_split(_dot(kk * kk, seg), segt)
    kk = kk / jnp.maximum(jnp.sqrt(sumsq), 1e-12)
    k = k * (1.0 + (a - 1.0) * ka_ref[...])
    rk = _dot_lhs_split(_dot(r * k * rk_ref[...], seg), segt)

    r_out[0] = r.astype(BF16)
    k_out[0] = k.astype(BF16)
    v_out[0] = v.astype(BF16)
    g_out[0] = g.astype(BF16)
    kkn_out[0] = kk.astype(BF16)
    beta_out[0] = (kk * a).astype(BF16)
    logw_out[0] = logw
    bonus_out[0] = (rk * v).astype(BF16)


def _rwkv_in(x, norm_pre, mu, w_in, w0, w1, w2, a0, a1, a2, k_k, k_a, r_k, seg, segt, tile):
    b, s, d = x.shape
    lora = w1.shape[1]
    tok = pl.BlockSpec((1, tile, d), lambda i, j: (i, j, 0))
    prev = pl.BlockSpec((1, 8, d), lambda i, j: (i, jnp.maximum(j * (tile // 8) - 1, 0), 0))
    row = _const_spec((1, d))
    out_bf = jax.ShapeDtypeStruct((b, s, d), BF16)
    out_f32 = jax.ShapeDtypeStruct((b, s, d), F32)
    return pl.pallas_call(
        _rwkv_in_kernel,
        grid=(b, s // tile),
        in_specs=[tok, prev, row, _const_spec((6, d)), _const_spec((4, d, d)), row,
                  _const_spec((d, lora)), _const_spec((lora, d)), row, _const_spec((d, lora)),
                  _const_spec((lora, d)), row, row, row, _const_spec((d, PAIR)), _const_spec((PAIR, d))],
        out_specs=[tok] * 8,
        out_shape=[out_bf] * 6 + [out_f32, out_bf],
        compiler_params=pltpu.CompilerParams(dimension_semantics=("parallel", "parallel"),
                                             vmem_limit_bytes=VMEM_LIMIT),
        name="rwkv_in",
    )(x, x, norm_pre, mu, w_in, w0, w1, w2, a0, a1, a2, k_k, k_a, r_k, seg, segt)


def _rwkv_scan_kernel(r_ref, k_ref, v_ref, kk_ref, beta_ref, logw_ref, y_ref, z_ref):
    @pl.when(pl.program_id(1) == 0)
    def _():
        z_ref[...] = jnp.zeros_like(z_ref)

    t, d = r_ref.shape[1], r_ref.shape[2]
    c = CHUNK
    pairs = d // PAIR
    probs = [(ch, h) for ch in range(t // c) for h in range(pairs)]

    ri = lax.broadcasted_iota(jnp.int32, (PAIR, PAIR), 0)
    ci = lax.broadcasted_iota(jnp.int32, (PAIR, PAIR), 1)
    same_head = (ri // HEAD) == (ci // HEAD)
    strict = ri > ci
    eye = ri == ci
    ti = lax.broadcasted_iota(jnp.int32, (c, 2 * PAIR), 0)
    si = lax.broadcasted_iota(jnp.int32, (c, 2 * PAIR), 1)
    incl = (si % HEAD) <= ti
    head0 = lax.broadcasted_iota(jnp.int32, (c, PAIR), 1) < HEAD
    tri = jnp.where(lax.broadcasted_iota(jnp.int32, (c, c), 1)
                    <= lax.broadcasted_iota(jnp.int32, (c, c), 0), 1.0, 0.0).astype(BF16)

    def stack_heads(m):
        return jnp.concatenate([jnp.where(head0, m, 0.0), jnp.where(head0, 0.0, m)], axis=0)

    def load(ref, ch, h):
        return ref[0, pl.ds(ch * c, c), pl.ds(h * PAIR, PAIR)]

    lws = [load(logw_ref, ch, h) for ch, h in probs]
    cums = []
    for lw in lws:
        hi, lo = _split(lw)
        both = _dot(tri, jnp.concatenate([hi, lo], axis=1))
        cums.append(both[:, :PAIR] + both[:, PAIR:])
    ops = []
    for (ch, h), lw, cum in zip(probs, lws, cums):
        r = load(r_ref, ch, h).astype(F32)
        k = load(k_ref, ch, h).astype(F32)
        v = load(v_ref, ch, h).astype(F32)
        kk = load(kk_ref, ch, h).astype(F32)
        beta = load(beta_ref, ch, h).astype(F32)
        cum_end = cum[c - 1:c, :]
        e_neg = jnp.exp(-cum)
        e_end = jnp.exp(cum_end - cum)
        ops.append(dict(
            r_t=r * jnp.exp(cum), xa=stack_heads(-kk * jnp.exp(cum - lw)).astype(BF16),
            bk_h=jnp.concatenate([stack_heads(beta * e_neg), stack_heads(k * e_neg)], axis=0).astype(BF16),
            b_e=(beta * e_end).astype(BF16), k_e=(k * e_end).astype(BF16),
            v=v.astype(BF16), vst=stack_heads(v).astype(BF16), w_end=jnp.exp(cum_end)))

    for o in ops:
        g = _dot_nt(jnp.concatenate([o["xa"], o["r_t"].astype(BF16)], axis=0), o["bk_h"])
        o["a_ab"] = jnp.where(strict, g[:PAIR, :PAIR], 0.0)
        o["a_ak"] = jnp.where(strict, g[:PAIR, PAIR:], 0.0).astype(BF16)
        o["a_q"] = jnp.where(incl, g[PAIR:, :], 0.0).astype(BF16)
    for o in ops:
        o["akv"] = _dot(o["a_ak"], o["vst"])

    for o in ops:
        o["tinv"] = jnp.where(eye, 1.0, 0.0) + jnp.where((ri // 2 == ci // 2), o["a_ab"], 0.0)
    half = 2
    while half < c:
        blk = 2 * half
        sel = (ri // blk == ci // blk) & (ri % blk >= half) & (ci % blk < half)
        for o in ops:
            o["tmp"] = _dot(o["tinv"], jnp.where(sel, o["a_ab"], 0.0))
        for o in ops:
            o["tinv"] = o["tinv"] + _dot(o["tmp"], o["tinv"])
        half = blk

    for o in ops:
        o["wu_st"] = _dot(o["tinv"], jnp.concatenate([o["xa"], o["akv"].astype(BF16)], axis=1))
    for o in ops:
        o["rq_y"] = _dot(o["a_q"][:, :PAIR], o["wu_st"])
        o["y_kv"] = _dot(o["a_q"][:, PAIR:], o["vst"])
    for o in ops:
        wu = o["wu_st"][:c] + o["wu_st"][c:]
        o["pq"] = _dot_tn(o["b_e"], wu)
        o["q_kv"] = _dot_tn(o["k_e"], o["v"])
    for o in ops:
        o["rq"] = o["r_t"] + o["rq_y"][:, :PAIR]
        o["y_in"] = o["rq_y"][:, PAIR:] + o["y_kv"]
        o["p_t"] = jnp.where(eye, o["w_end"], 0.0) + jnp.where(same_head, o["pq"][:, :PAIR], 0.0)
        o["q_t"] = jnp.where(same_head, o["pq"][:, PAIR:] + o["q_kv"], 0.0)

    zs = [z_ref[h] for h in range(pairs)]
    for ch in range(t // c):
        chunk_ops = ops[ch * pairs:(ch + 1) * pairs]
        z_hl = [_split(z) for z in zs]
        for h, (o, (zh, zl)) in enumerate(zip(chunk_ops, z_hl)):
            y = _dot(o["rq"], zh) + o["y_in"]
            y_ref[0, pl.ds(ch * c, c), pl.ds(h * PAIR, PAIR)] = y.astype(y_ref.dtype)
        new = []
        for o, (zh, zl) in zip(chunk_ops, z_hl):
            ph, plo = _split(o["p_t"])
            new.append(_dot(ph, zh) + (_dot(ph, zl) + _dot(plo, zh)) + o["q_t"])
        zs = new
    for h in range(pairs):
        z_ref[h] = zs[h]


def _rwkv_scan(r, k, v, kk, beta, logw, tile):
    b, s, d = r.shape
    tok = pl.BlockSpec((1, tile, d), lambda i, j: (i, j, 0))
    return pl.pallas_call(
        _rwkv_scan_kernel,
        grid=(b, s // tile),
        in_specs=[tok] * 6,
        out_specs=tok,
        out_shape=jax.ShapeDtypeStruct((b, s, d), BF16),
        scratch_shapes=[pltpu.VMEM((d // PAIR, PAIR, PAIR), F32)],
        compiler_params=pltpu.CompilerParams(
            dimension_semantics=("parallel", "arbitrary"), vmem_limit_bytes=VMEM_LIMIT),
        name="rwkv_scan",
    )(r, k, v, kk, beta, logw)


def _ple(h, p, ple_w, ple_gate, ple_norm):
    e = _dot(p, ple_w)
    gate = _sigmoid(_dot(h, ple_gate))
    return h + _rms(gate * e, ple_norm, NORM_EPS)


def _mid_kernel(y_ref, g_ref, bonus_ref, x_ref, p_ref, seg_ref, segt_ref, lng_ref, lnb_ref, wout_ref,
                npost_ref, plew_ref, pleg_ref, plen_ref, kvn_ref, kvw_ref, npre1_ref, bwin_ref,
                cos_ref, s1_ref, s2_ref,
                h_out, q_out, k_out, v_out, gate_out):
    d = x_ref.shape[-1]
    seg = seg_ref[...]
    segt = segt_ref[...]
    y = y_ref[0].astype(F32)
    mean = _dot_lhs_split(_dot_lhs_split(y, seg), segt) * (1.0 / HEAD)
    yc = y - mean
    var = _dot_lhs_split(_dot(yc * yc, seg), segt) * (1.0 / HEAD)
    yn = yc * lax.rsqrt(var + GN_EPS) * lng_ref[...] + lnb_ref[...]
    yn = yn + bonus_ref[0].astype(F32)
    g = g_ref[0].astype(F32)
    mixed = _dot(yn * (g * _sigmoid(g)), wout_ref[...])
    h = x_ref[0] + _rms(mixed, npost_ref[...], NORM_EPS)
    h = _ple(h, p_ref[0, 0], plew_ref[...], pleg_ref[...], plen_ref[...])
    h_out[0] = h

    cos, s1, s2 = cos_ref[...], s1_ref[...], s2_ref[...]
    kv = _dot(_rms(h, kvn_ref[...], NORM_EPS), kvw_ref[...])
    k_out[0] = _rope(kv[:, :d], cos, s1, s2).astype(BF16)
    v_out[0] = kv[:, d:].astype(BF16)
    proj = _dot(_rms(h, npre1_ref[...], NORM_EPS), bwin_ref[...])
    q_out[0] = (_rope(proj[:, :d], cos, s1, s2) * (HEAD ** -0.5 * LOG2E)).astype(BF16)
    gate_out[0] = proj[:, d:].astype(BF16)


def _mid(y, g, bonus, x, p, seg, segt, lnx_g, lnx_b, w_out, norm_post, ple_w, ple_gate, ple_norm,
         kv_norm, kv_w, norm_pre1, b_w_in, cos, s1, s2, tile):
    b, s, d = x.shape
    pd = p.shape[-1]
    tok = pl.BlockSpec((1, tile, d), lambda i, j: (i, j, 0))
    ptok = pl.BlockSpec((1, 1, tile, pd), lambda i, j: (0, i, j, 0))
    tab = pl.BlockSpec((tile, PAIR), lambda i, j: (j, 0))
    row = _const_spec((1, d))
    out_bf = jax.ShapeDtypeStruct((b, s, d), BF16)
    return pl.pallas_call(
        _mid_kernel,
        grid=(b, s // tile),
        in_specs=[tok, tok, tok, tok, ptok, _const_spec((d, PAIR)), _const_spec((PAIR, d)), row, row,
                  _const_spec((d, d)), row, _const_spec((pd, d)), _const_spec((d, d)), row, row,
                  _const_spec((d, 2 * d)), row, _const_spec((d, 2 * d)), tab, tab, tab],
        out_specs=[tok] * 5,
        out_shape=[jax.ShapeDtypeStruct((b, s, d), F32)] + [out_bf] * 4,
        compiler_params=pltpu.CompilerParams(dimension_semantics=("parallel", "parallel"),
                                             vmem_limit_bytes=VMEM_LIMIT),
        name="mid",
    )(y, g, bonus, x, p, seg, segt, lnx_g, lnx_b, w_out, norm_post, ple_w, ple_gate, ple_norm,
      kv_norm, kv_w, norm_pre1, b_w_in, cos, s1, s2)


def _make_diff_attn_kernel(tq, tk, rows):
    reps = tk // PAIR
    assert tq % rows == 0

    def kernel(q_ref, k_ref, v_ref, lam_ref, subln_ref, o_ref, qs_ref, m_ref, acc_ref):
        i = pl.program_id(2)
        q = q_ref[0]
        lane = lax.broadcasted_iota(jnp.int32, q.shape, 1)
        zero = jnp.zeros_like(q)
        qs_ref[...] = jnp.concatenate([jnp.where(lane < HEAD, q, zero), jnp.where(lane < HEAD, zero, q)],
                                      axis=0)
        m_ref[...] = jnp.full_like(m_ref, MASK_VALUE)
        acc_ref[...] = jnp.zeros_like(acc_ref)
        ones = jnp.ones((tk, PAIR), BF16)

        def step(j, masked):
            keys = pl.ds(pl.multiple_of(j * tk, tk), tk)
            k_blk = k_ref[0, keys, :]
            v_ext = jnp.concatenate([v_ref[0, keys, :], ones], axis=1)
            groups = [pl.ds(g * rows, rows) for g in range(2 * tq // rows)]
            ss = [_dot_nt(qs_ref[g, :], k_blk) for g in groups]
            if masked:
                qpos = i * tq + lax.broadcasted_iota(jnp.int32, (rows, tk), 0)
                kpos = j * tk + lax.broadcasted_iota(jnp.int32, (rows, tk), 1)
                ss = [jnp.where(kpos <= qpos + (n * rows) % tq, s, MASK_VALUE) for n, s in enumerate(ss)]
            m_olds = [m_ref[g, :] for g in groups]
            m_news = [jnp.maximum(m_old, jnp.max(s, axis=-1, keepdims=True)) for m_old, s in zip(m_olds, ss)]
            ps = [jnp.exp2(s - jnp.tile(m_new, (1, reps))).astype(BF16) for s, m_new in zip(ss, m_news)]
            pvs = [_dot(p, v_ext) for p in ps]
            for g, m_old, m_new, pv in zip(groups, m_olds, m_news, pvs):
                scale = jnp.exp2(m_old - m_new)
                acc_ref[g, :] = jnp.tile(scale, (1, 2)) * acc_ref[g, :] + pv
                m_ref[g, :] = m_new

        def body(j, carry):
            step(j, False)
            return carry

        diag = (i * tq) // tk
        lax.fori_loop(0, diag, body, 0)
        step(diag, True)

        lq = lam_ref[...]
        lam = (jnp.exp(jnp.sum(lq[0:1] * lq[1:2], axis=-1, keepdims=True))
               - jnp.exp(jnp.sum(lq[2:3] * lq[3:4], axis=-1, keepdims=True)) + LAM_INIT_1)
        acc = acc_ref[...]
        o = acc[:, :PAIR] / acc[:, PAIR:]
        o = o[:tq] - lam * o[tq:]
        o = _rms(o, subln_ref[...], SUBLN_EPS) * (1.0 - LAM_INIT_1)
        o_ref[0] = o.astype(o_ref.dtype)

    return kernel


def _diff_attn(q, k, v, lam_qk, subln, tq, tk, rows):
    b, s, d = q.shape
    qtok = pl.BlockSpec((1, tq, PAIR), lambda i, h, j: (i, j, h))
    seq = pl.BlockSpec((1, s, PAIR), lambda i, h, j: (i, 0, h))
    return pl.pallas_call(
        _make_diff_attn_kernel(tq, tk, rows),
        grid=(b, d // PAIR, s // tq),
        in_specs=[qtok, seq, seq, _const_spec(lam_qk.shape), _const_spec((1, PAIR))],
        out_specs=qtok,
        out_shape=jax.ShapeDtypeStruct((b, s, d), BF16),
        scratch_shapes=[pltpu.VMEM((2 * tq, PAIR), BF16), pltpu.VMEM((2 * tq, PAIR), F32),
                        pltpu.VMEM((2 * tq, 2 * PAIR), F32)],
        compiler_params=pltpu.CompilerParams(
            dimension_semantics=("parallel", "parallel", "arbitrary"), vmem_limit_bytes=VMEM_LIMIT),
        name="diff_attn",
    )(q, k, v, lam_qk, subln)


def _attn_out_kernel(o_ref, gate_ref, h_ref, p_ref, wout_ref, npost_ref, plew_ref, pleg_ref, plen_ref,
                     out_ref):
    g = gate_ref[0].astype(F32)
    mixed = _dot(o_ref[0].astype(F32) * (g * _sigmoid(g)), wout_ref[...])
    h = h_ref[0] + _rms(mixed, npost_ref[...], NORM_EPS)
    out_ref[0] = _ple(h, p_ref[0, 0], plew_ref[...], pleg_ref[...], plen_ref[...])


def _attn_out(o, gate, h, p, w_out, norm_post, ple_w, ple_gate, ple_norm, tile):
    b, s, d = h.shape
    pd = p.shape[-1]
    tok = pl.BlockSpec((1, tile, d), lambda i, j: (i, j, 0))
    ptok = pl.BlockSpec((1, 1, tile, pd), lambda i, j: (1, i, j, 0))
    row = _const_spec((1, d))
    return pl.pallas_call(
        _attn_out_kernel,
        grid=(b, s // tile),
        in_specs=[tok, tok, tok, ptok, _const_spec((d, d)), row, _const_spec((pd, d)),
                  _const_spec((d, d)), row],
        out_specs=tok,
        out_shape=jax.ShapeDtypeStruct((b, s, d), F32),
        compiler_params=pltpu.CompilerParams(dimension_semantics=("parallel", "parallel"),
                                             vmem_limit_bytes=VMEM_LIMIT),
        name="attn_out",
    )(o, gate, h, p, w_out, norm_post, ple_w, ple_gate, ple_norm)


def _rope_patterns(s):
    half = ROPE_DIMS // 2
    inv = ROPE_THETA ** (-jnp.arange(0, ROPE_DIMS, 2, dtype=F32) / ROPE_DIMS)
    ang = jnp.arange(s, dtype=F32)[:, None] * inv[None, :]
    cos, sin = jnp.cos(ang), jnp.sin(ang)
    lane = jnp.arange(PAIR) % HEAD
    idx = lane % half
    c = jnp.where(lane < ROPE_DIMS, cos[:, idx], 1.0)
    s1 = jnp.where(lane < half, -sin[:, idx], 0.0)
    s2 = jnp.where((lane >= half) & (lane < ROPE_DIMS), sin[:, idx], 0.0)
    return c, s1, s2


def kernel(x, p, norm_pre, norm_post, a_mu, a_w_in, a_w0, a_w1, a_w2, a_a0, a_a1, a_a2, a_k_k, a_k_a,
           a_r_k, a_lnx_g, a_lnx_b, a_w_out, kv_norm, kv_w, b_w_in, b_lambda, b_subln, b_w_out, ple_w,
           ple_gate, ple_norm):
    b, s, d = x.shape
    assert d % PAIR == 0
    t_proj, t_scan, t_q, t_kv, t_rows = _tiles(s)
    row = lambda t: t.reshape(1, d).astype(F32)
    bf = lambda t: t.astype(BF16)

    head_of_lane = jnp.arange(d)[:, None] // HEAD
    seg = (head_of_lane == jnp.arange(PAIR)[None, :]).astype(BF16)
    segt = seg.T
    cos, s1, s2 = _rope_patterns(s)

    r, k, v, g, kk, beta, logw, bonus = _rwkv_in(
        x, row(norm_pre[0]), a_mu[0], bf(a_w_in[0]), row(a_w0[0]), bf(a_w1[0]), bf(a_w2[0]),
        row(a_a0[0]), bf(a_a1[0]), bf(a_a2[0]), row(a_k_k[0]), row(a_k_a[0]), row(a_r_k[0]),
        seg, segt, t_proj)
    y = _rwkv_scan(r, k, v, kk, beta, logw, t_scan)
    h, q, kr, vv, gate = _mid(
        y, g, bonus, x, p, seg, segt, row(a_lnx_g[0]), row(a_lnx_b[0]), bf(a_w_out[0]),
        row(norm_post[0]), bf(ple_w[0]), bf(ple_gate[0]), row(ple_norm[0]), row(kv_norm), bf(kv_w),
        row(norm_pre[1]), bf(b_w_in[0]), cos, s1, s2, t_proj)
    o = _diff_attn(q, kr, vv, b_lambda[0].astype(F32), b_subln[0].reshape(1, PAIR).astype(F32), t_q, t_kv,
                   t_rows)
    return _attn_out(o, gate, h, p, bf(b_w_out[0]), row(norm_post[1]), bf(ple_w[1]), bf(ple_gate[1]),
                     row(ple_norm[1]), t_proj)
```

```python
import math

import jax
import jax.numpy as jnp
from jax import lax
from jax.experimental import pallas as pl
from jax.experimental.pallas import tpu as pltpu

F32 = jnp.float32
BF16 = jnp.bfloat16

HEAD = 64
PAIR = 2 * HEAD
CHUNK = 64
ROPE_DIMS = 16
ROPE_THETA = 500000.0
NORM_EPS = 1e-6
SUBLN_EPS = 1e-5
GN_EPS = 64e-5
LAM_INIT_1 = 0.8 - 0.6 * math.exp(-0.3 * 1)
DECAY_SCALE = math.exp(-0.5)
LOG2E = math.log2(math.e)
MASK_VALUE = -1e30
VMEM_LIMIT = 56 * 1024 * 1024


def _tiles(s):
    proj = min(256, s)
    scan = min(2 * CHUNK, s)
    kv = min(1024, s)
    q = min(1024, s)
    rows = min(256, q)
    assert s % proj == 0 and s % scan == 0 and s % kv == 0 and kv % q == 0 and scan % CHUNK == 0
    return proj, scan, q, kv, rows


def _dot(a, b):
    return jnp.dot(a.astype(BF16), b.astype(BF16), preferred_element_type=F32)


def _dot_nt(a, b):
    return lax.dot_general(a.astype(BF16), b.astype(BF16), (((1,), (1,)), ((), ())),
                           preferred_element_type=F32)


def _dot_tn(a, b):
    return lax.dot_general(a.astype(BF16), b.astype(BF16), (((0,), (0,)), ((), ())),
                           preferred_element_type=F32)


def _split(a):
    hi = a.astype(BF16)
    lo = (a - hi.astype(F32)).astype(BF16)
    return hi, lo


def _rms(x, g, eps):
    return x * lax.rsqrt(jnp.mean(x * x, axis=-1, keepdims=True) + eps) * g


def _sigmoid(x):
    return 1.0 / (1.0 + jnp.exp(-x))


def _rope(x, c, s1, s2):
    d = x.shape[-1]
    rep = d // PAIR
    c = jnp.tile(c, (1, rep))
    s1 = jnp.tile(s1, (1, rep))
    s2 = jnp.tile(s2, (1, rep))
    up = pltpu.roll(x, d - ROPE_DIMS // 2, 1)
    dn = pltpu.roll(x, ROPE_DIMS // 2, 1)
    return x * c + up * s1 + dn * s2


def _row_groups(t, n=2):
    return [pl.ds(i * (t // n), t // n) for i in range(n)] if t % (8 * n) == 0 else [pl.ds(0, t)]


def _const_spec(shape):
    nd = len(shape)
    return pl.BlockSpec(shape, lambda *_: (0,) * nd)


def _rwkv_in_kernel(x_ref, xp_ref, g_ref, mu_ref, win_ref, w0_ref, w1_ref, w2_ref, a0_ref, a1_ref,
                    a2_ref, kk_ref, ka_ref, rk_ref, seg_ref, segt_ref,
                    r_out, k_out, v_out, g_out, kkn_out, beta_out, logw_out, bonus_out):
    s = pl.program_id(1)
    gain = g_ref[...]
    xn = _rms(x_ref[0], gain, NORM_EPS)
    t = xn.shape[0]
    prev_last = _rms(xp_ref[0], gain, NORM_EPS)[7:8]
    prev_last = jnp.where(s > 0, prev_last, 0.0)
    row = lax.broadcasted_iota(jnp.int32, (t, 1), 0)
    xprev = jnp.where(row == 0, prev_last, pltpu.roll(xn, 1, 0))
    dx = xprev - xn
    mix = lambda c: xn + dx * mu_ref[c:c + 1, :]

    r = _dot(mix(0), win_ref[0])
    k = _dot(mix(1), win_ref[1])
    v = _dot(mix(2), win_ref[2])
    g = _dot(mix(3), win_ref[3])
    w_pre = w0_ref[...] + _dot(jnp.tanh(_dot(mix(4), w1_ref[...])), w2_ref[...])
    logw = -DECAY_SCALE * _sigmoid(w_pre)
    a = _sigmoid(a0_ref[...] + _dot(_dot(mix(5), a1_ref[...]), a2_ref[...]))

    seg = seg_ref[...]
    segt = segt_ref[...]
    kk = k * kk_ref[...]
    sumsq = _dot(_dot(kk * kk, seg), segt)
    kk = kk / jnp.maximum(jnp.sqrt(sumsq), 1e-12)
    k = k * (1.0 + (a - 1.0) * ka_ref[...])
    rk = _dot(_dot(r * k * rk_ref[...], seg), segt)

    r_out[0] = r.astype(BF16)
    k_out[0] = k.astype(BF16)
    v_out[0] = v.astype(BF16)
    g_out[0] = g.astype(BF16)
    kkn_out[0] = kk.astype(BF16)
    beta_out[0] = (kk * a).astype(BF16)
    logw_out[0] = logw
    bonus_out[0] = (rk * v).astype(BF16)


def _rwkv_in(x, norm_pre, mu, w_in, w0, w1, w2, a0, a1, a2, k_k, k_a, r_k, seg, segt, tile):
    b, s, d = x.shape
    lora = w1.shape[1]
    tok = pl.BlockSpec((1, tile, d), lambda i, j: (i, j, 0))
    prev = pl.BlockSpec((1, 8, d), lambda i, j: (i, jnp.maximum(j * (tile // 8) - 1, 0), 0))
    row = _const_spec((1, d))
    out_bf = jax.ShapeDtypeStruct((b, s, d), BF16)
    out_f32 = jax.ShapeDtypeStruct((b, s, d), F32)
    return pl.pallas_call(
        _rwkv_in_kernel,
        grid=(b, s // tile),
        in_specs=[tok, prev, row, _const_spec((6, d)), _const_spec((4, d, d)), row,
                  _const_spec((d, lora)), _const_spec((lora, d)), row, _const_spec((d, lora)),
                  _const_spec((lora, d)), row, row, row, _const_spec((d, PAIR)), _const_spec((PAIR, d))],
        out_specs=[tok] * 8,
        out_shape=[out_bf] * 6 + [out_f32, out_bf],
        compiler_params=pltpu.CompilerParams(dimension_semantics=("parallel", "parallel"),
                                             vmem_limit_bytes=VMEM_LIMIT),
        name="rwkv_in",
    )(x, x, norm_pre, mu, w_in, w0, w1, w2, a0, a1, a2, k_k, k_a, r_k, seg, segt)


def _rwkv_scan_kernel(r_ref, k_ref, v_ref, kk_ref, beta_ref, logw_ref, y_ref, z_ref):
    @pl.when(pl.program_id(1) == 0)
    def _():
        z_ref[...] = jnp.zeros_like(z_ref)

    t, d = r_ref.shape[1], r_ref.shape[2]
    c = CHUNK
    pairs = d // PAIR
    probs = [(ch, h) for ch in range(t // c) for h in range(pairs)]

    ri = lax.broadcasted_iota(jnp.int32, (PAIR, PAIR), 0)
    ci = lax.broadcasted_iota(jnp.int32, (PAIR, PAIR), 1)
    same_head = (ri // HEAD) == (ci // HEAD)
    strict = ri > ci
    eye = ri == ci
    ti = lax.broadcasted_iota(jnp.int32, (c, 2 * PAIR), 0)
    si = lax.broadcasted_iota(jnp.int32, (c, 2 * PAIR), 1)
    incl = (si % HEAD) <= ti
    head0 = lax.broadcasted_iota(jnp.int32, (c, PAIR), 1) < HEAD
    tri = jnp.where(lax.broadcasted_iota(jnp.int32, (c, c), 1)
                    <= lax.broadcasted_iota(jnp.int32, (c, c), 0), 1.0, 0.0).astype(BF16)

    def stack_heads(m):
        return jnp.concatenate([jnp.where(head0, m, 0.0), jnp.where(head0, 0.0, m)], axis=0)

    def load(ref, ch, h):
        return ref[0, pl.ds(ch * c, c), pl.ds(h * PAIR, PAIR)]

    lws = [load(logw_ref, ch, h) for ch, h in probs]
    cums = []
    for lw in lws:
        hi, lo = _split(lw)
        both = _dot(tri, jnp.concatenate([hi, lo], axis=1))
        cums.append(both[:, :PAIR] + both[:, PAIR:])
    ops = []
    for (ch, h), lw, cum in zip(probs, lws, cums):
        r = load(r_ref, ch, h).astype(F32)
        k = load(k_ref, ch, h).astype(F32)
        v = load(v_ref, ch, h).astype(F32)
        kk = load(kk_ref, ch, h).astype(F32)
        beta = load(beta_ref, ch, h).astype(F32)
        cum_end = cum[c - 1:c, :]
        e_neg = jnp.exp(-cum)
        e_end = jnp.exp(cum_end - cum)
        ops.append(dict(
            r_t=r * jnp.exp(cum), xa=stack_heads(-kk * jnp.exp(cum - lw)).astype(BF16),
            bk_h=jnp.concatenate([stack_heads(beta * e_neg), stack_heads(k * e_neg)], axis=0).astype(BF16),
            b_e=(beta * e_end).astype(BF16), k_e=(k * e_end).astype(BF16),
            v=v.astype(BF16), vst=stack_heads(v).astype(BF16), w_end=jnp.exp(cum_end)))

    for o in ops:
        g = _dot_nt(jnp.concatenate([o["xa"], o["r_t"].astype(BF16)], axis=0), o["bk_h"])
        o["a_ab"] = jnp.where(strict, g[:PAIR, :PAIR], 0.0)
        o["a_ak"] = jnp.where(strict, g[:PAIR, PAIR:], 0.0).astype(BF16)
        o["a_q"] = jnp.where(incl, g[PAIR:, :], 0.0).astype(BF16)
    for o in ops:
        o["akv"] = _dot(o["a_ak"], o["vst"])

    for o in ops:
        o["tinv"] = jnp.where(eye, 1.0, 0.0) + jnp.where((ri // 2 == ci // 2), o["a_ab"], 0.0)
    half = 2
    while half < c:
        blk = 2 * half
        sel = (ri // blk == ci // blk) & (ri % blk >= half) & (ci % blk < half)
        for o in ops:
            o["tmp"] = _dot(o["tinv"], jnp.where(sel, o["a_ab"], 0.0))
        for o in ops:
            o["tinv"] = o["tinv"] + _dot(o["tmp"], o["tinv"])
        half = blk

    for o in ops:
        o["wu_st"] = _dot(o["tinv"], jnp.concatenate([o["xa"], o["akv"].astype(BF16)], axis=1))
    for o in ops:
        o["rq_y"] = _dot(o["a_q"][:, :PAIR], o["wu_st"])
        o["y_kv"] = _dot(o["a_q"][:, PAIR:], o["vst"])
    for o in ops:
        wu = o["wu_st"][:c] + o["wu_st"][c:]
        o["pq"] = _dot_tn(o["b_e"], wu)
        o["q_kv"] = _dot_tn(o["k_e"], o["v"])
    for o in ops:
        o["rq"] = o["r_t"] + o["rq_y"][:, :PAIR]
        o["y_in"] = o["rq_y"][:, PAIR:] + o["y_kv"]
        o["p_t"] = jnp.where(eye, o["w_end"], 0.0) + jnp.where(same_head, o["pq"][:, :PAIR], 0.0)
        o["q_t"] = jnp.where(same_head, o["pq"][:, PAIR:] + o["q_kv"], 0.0)

    zs = [z_ref[h] for h in range(pairs)]
    for ch in range(t // c):
        chunk_ops = ops[ch * pairs:(ch + 1) * pairs]
        zbs = [z.astype(BF16) for z in zs]
        for h, (o, zb) in enumerate(zip(chunk_ops, zbs)):
            y = _dot(o["rq"], zb) + o["y_in"]
            y_ref[0, pl.ds(ch * c, c), pl.ds(h * PAIR, PAIR)] = y.astype(y_ref.dtype)
        new = []
        for o, zb in zip(chunk_ops, zbs):
            ph, plo = _split(o["p_t"])
            new.append(_dot(ph, zb) + _dot(plo, zb) + o["q_t"])
        zs = new
    for h in range(pairs):
        z_ref[h] = zs[h]


def _rwkv_scan(r, k, v, kk, beta, logw, tile):
    b, s, d = r.shape
    tok = pl.BlockSpec((1, tile, d), lambda i, j: (i, j, 0))
    return pl.pallas_call(
        _rwkv_scan_kernel,
        grid=(b, s // tile),
        in_specs=[tok] * 6,
        out_specs=tok,
        out_shape=jax.ShapeDtypeStruct((b, s, d), BF16),
        scratch_shapes=[pltpu.VMEM((d // PAIR, PAIR, PAIR), F32)],
        compiler_params=pltpu.CompilerParams(
            dimension_semantics=("parallel", "arbitrary"), vmem_limit_bytes=VMEM_LIMIT),
        name="rwkv_scan",
    )(r, k, v, kk, beta, logw)


def _ple_groups(hs, ps, plew_ref, pleg_ref, plen_ref):
    es = [_dot(p, plew_ref[...]) for p in ps]
    gates = [_dot(h, pleg_ref[...]) for h in hs]
    return [h + _rms(_sigmoid(gate) * e, plen_ref[...], NORM_EPS) for h, gate, e in zip(hs, gates, es)]


def _mid_kernel(y_ref, g_ref, bonus_ref, x_ref, p_ref, seg_ref, segt_ref, lng_ref, lnb_ref, wout_ref,
                npost_ref, plew_ref, pleg_ref, plen_ref, kvn_ref, kvw_ref, npre1_ref, bwin_ref,
                cos_ref, s1_ref, s2_ref,
                h_out, q_out, k_out, v_out, gate_out):
    d = x_ref.shape[-1]
    groups = _row_groups(x_ref.shape[1])
    both = lambda f: [f(rows) for rows in groups]

    ys = both(lambda rows: y_ref[0, rows, :])
    means = [_dot(_dot(y, seg_ref[...]), segt_ref[...]) * (1.0 / HEAD) for y in ys]
    ycs = [y.astype(F32) - mean for y, mean in zip(ys, means)]
    vars_ = [_dot(_dot(yc * yc, seg_ref[...]), segt_ref[...]) * (1.0 / HEAD) for yc in ycs]
    zs = []
    for rows, yc, var in zip(groups, ycs, vars_):
        yn = yc * lax.rsqrt(var + GN_EPS) * lng_ref[...] + lnb_ref[...] + bonus_ref[0, rows, :].astype(F32)
        g = g_ref[0, rows, :].astype(F32)
        zs.append((yn * (g * _sigmoid(g))).astype(BF16))
    mixed = [_dot(z, wout_ref[...]) for z in zs]
    hs = [x_ref[0, rows, :] + _rms(m, npost_ref[...], NORM_EPS) for rows, m in zip(groups, mixed)]
    hs = _ple_groups(hs, both(lambda rows: p_ref[0, 0, rows, :]), plew_ref, pleg_ref, plen_ref)
    for rows, h in zip(groups, hs):
        h_out[0, rows, :] = h

    tabs = both(lambda rows: (cos_ref[rows, :], s1_ref[rows, :], s2_ref[rows, :]))
    kvs = [_dot(_rms(h, kvn_ref[...], NORM_EPS), kvw_ref[...]) for h in hs]
    projs = [_dot(_rms(h, npre1_ref[...], NORM_EPS), bwin_ref[...]) for h in hs]
    for rows, kv, proj, tab in zip(groups, kvs, projs, tabs):
        k_out[0, rows, :] = _rope(kv[:, :d], *tab).astype(BF16)
        v_out[0, rows, :] = kv[:, d:].astype(BF16)
        q_out[0, rows, :] = (_rope(proj[:, :d], *tab) * (HEAD ** -0.5 * LOG2E)).astype(BF16)
        gate_out[0, rows, :] = proj[:, d:].astype(BF16)


def _mid(y, g, bonus, x, p, seg, segt, lnx_g, lnx_b, w_out, norm_post, ple_w, ple_gate, ple_norm,
         kv_norm, kv_w, norm_pre1, b_w_in, cos, s1, s2, tile):
    b, s, d = x.shape
    pd = p.shape[-1]
    tok = pl.BlockSpec((1, tile, d), lambda i, j: (i, j, 0))
    ptok = pl.BlockSpec((1, 1, tile, pd), lambda i, j: (0, i, j, 0))
    tab = pl.BlockSpec((tile, PAIR), lambda i, j: (j, 0))
    row = _const_spec((1, d))
    out_bf = jax.ShapeDtypeStruct((b, s, d), BF16)
    return pl.pallas_call(
        _mid_kernel,
        grid=(b, s // tile),
        in_specs=[tok, tok, tok, tok, ptok, _const_spec((d, PAIR)), _const_spec((PAIR, d)), row, row,
                  _const_spec((d, d)), row, _const_spec((pd, d)), _const_spec((d, d)), row, row,
                  _const_spec((d, 2 * d)), row, _const_spec((d, 2 * d)), tab, tab, tab],
        out_specs=[tok] * 5,
        out_shape=[jax.ShapeDtypeStruct((b, s, d), F32)] + [out_bf] * 4,
        compiler_params=pltpu.CompilerParams(dimension_semantics=("parallel", "parallel"),
                                             vmem_limit_bytes=VMEM_LIMIT),
        name="mid",
    )(y, g, bonus, x, p, seg, segt, lnx_g, lnx_b, w_out, norm_post, ple_w, ple_gate, ple_norm,
      kv_norm, kv_w, norm_pre1, b_w_in, cos, s1, s2)


def _make_diff_attn_kernel(tq, tk, rows):
    assert tq % rows == 0 and tk % tq == 0

    def kernel(q_ref, k_ref, v_ref, lam_ref, subln_ref, o_ref, qs_ref, m_ref, acc_ref):
        i = pl.program_id(2)
        q = q_ref[0]
        lane = lax.broadcasted_iota(jnp.int32, q.shape, 1)
        zero = jnp.zeros_like(q)
        qs_ref[...] = jnp.concatenate([jnp.where(lane < HEAD, q, zero), jnp.where(lane < HEAD, zero, q)],
                                      axis=0)
        groups = [pl.ds(g * rows, rows) for g in range(2 * tq // rows)]

        def step(start, width, diagonal):
            keys = pl.ds(pl.multiple_of(start, tq), width)
            k_blk = k_ref[0, keys, :]
            v_ext = jnp.concatenate([v_ref[0, keys, :], jnp.ones((width, PAIR), BF16)], axis=1)
            widths = [((n * rows) % tq + rows if diagonal else width) for n in range(len(groups))]
            ss = [_dot_nt(qs_ref[g, :], k_blk[:w]) for g, w in zip(groups, widths)]
            if diagonal:
                masks = {}
                for n, w in enumerate(widths):
                    off = (n * rows) % tq
                    if off not in masks:
                        masks[off] = (lax.broadcasted_iota(jnp.int32, (rows, w), 1)
                                      <= lax.broadcasted_iota(jnp.int32, (rows, w), 0) + off)
                    ss[n] = jnp.where(masks[off], ss[n], MASK_VALUE)
            if diagonal:
                m_news = [jnp.broadcast_to(jnp.max(s, axis=-1, keepdims=True), (rows, PAIR)) for s in ss]
            else:
                m_olds = [m_ref[g, :] for g in groups]
                m_news = [jnp.maximum(m_old, jnp.max(s, axis=-1, keepdims=True))
                          for m_old, s in zip(m_olds, ss)]
            ps = [jnp.exp2(s - jnp.tile(m_new, (1, w // PAIR))).astype(BF16)
                  for s, m_new, w in zip(ss, m_news, widths)]
            pvs = [_dot(p, v_ext[:w]) for p, w in zip(ps, widths)]
            for n, (g, m_new, pv) in enumerate(zip(groups, m_news, pvs)):
                if diagonal:
                    acc_ref[g, :] = pv
                else:
                    scale = jnp.exp2(m_olds[n] - m_new)
                    acc_ref[g, :] = jnp.tile(scale, (1, 2)) * acc_ref[g, :] + pv
                m_ref[g, :] = m_new

        def big(j, carry):
            step(j * tk, tk, False)
            return carry

        def small(j, carry):
            step(n_big * tk + j * tq, tq, False)
            return carry

        step(i * tq, tq, True)
        n_big = (i * tq) // tk
        lax.fori_loop(0, n_big, big, 0)
        if tk != tq:
            lax.fori_loop(0, i - n_big * (tk // tq), small, 0)

        lq = lam_ref[...]
        lam = (jnp.exp(jnp.sum(lq[0:1] * lq[1:2], axis=-1, keepdims=True))
               - jnp.exp(jnp.sum(lq[2:3] * lq[3:4], axis=-1, keepdims=True)) + LAM_INIT_1)
        acc = acc_ref[...]
        o = acc[:, :PAIR] / acc[:, PAIR:]
        o = o[:tq] - lam * o[tq:]
        o = _rms(o, subln_ref[...], SUBLN_EPS) * (1.0 - LAM_INIT_1)
        o_ref[0] = o.astype(o_ref.dtype)

    return kernel


def _diff_attn(q, k, v, lam_qk, subln, tq, tk, rows):
    b, s, d = q.shape
    qtok = pl.BlockSpec((1, tq, PAIR), lambda i, h, j: (i, j, h))
    seq = pl.BlockSpec((1, s, PAIR), lambda i, h, j: (i, 0, h))
    return pl.pallas_call(
        _make_diff_attn_kernel(tq, tk, rows),
        grid=(b, d // PAIR, s // tq),
        in_specs=[qtok, seq, seq, _const_spec(lam_qk.shape), _const_spec((1, PAIR))],
        out_specs=qtok,
        out_shape=jax.ShapeDtypeStruct((b, s, d), BF16),
        scratch_shapes=[pltpu.VMEM((2 * tq, PAIR), BF16), pltpu.VMEM((2 * tq, PAIR), F32),
                        pltpu.VMEM((2 * tq, 2 * PAIR), F32)],
        compiler_params=pltpu.CompilerParams(
            dimension_semantics=("parallel", "parallel", "arbitrary"), vmem_limit_bytes=VMEM_LIMIT),
        name="diff_attn",
    )(q, k, v, lam_qk, subln)


def _attn_out_kernel(o_ref, gate_ref, h_ref, p_ref, wout_ref, npost_ref, plew_ref, pleg_ref, plen_ref,
                     out_ref):
    groups = _row_groups(o_ref.shape[1])
    gs = [gate_ref[0, rows, :].astype(F32) for rows in groups]
    zs = [(o_ref[0, rows, :].astype(F32) * (g * _sigmoid(g))).astype(BF16) for rows, g in zip(groups, gs)]
    mixed = [_dot(z, wout_ref[...]) for z in zs]
    hs = [h_ref[0, rows, :] + _rms(m, npost_ref[...], NORM_EPS) for rows, m in zip(groups, mixed)]
    outs = _ple_groups(hs, [p_ref[0, 0, rows, :] for rows in groups], plew_ref, pleg_ref, plen_ref)
    for rows, out in zip(groups, outs):
        out_ref[0, rows, :] = out


def _attn_out(o, gate, h, p, w_out, norm_post, ple_w, ple_gate, ple_norm, tile):
    b, s, d = h.shape
    pd = p.shape[-1]
    tok = pl.BlockSpec((1, tile, d), lambda i, j: (i, j, 0))
    ptok = pl.BlockSpec((1, 1, tile, pd), lambda i, j: (1, i, j, 0))
    row = _const_spec((1, d))
    return pl.pallas_call(
        _attn_out_kernel,
        grid=(b, s // tile),
        in_specs=[tok, tok, tok, ptok, _const_spec((d, d)), row, _const_spec((pd, d)),
                  _const_spec((d, d)), row],
        out_specs=tok,
        out_shape=jax.ShapeDtypeStruct((b, s, d), F32),
        compiler_params=pltpu.CompilerParams(dimension_semantics=("parallel", "parallel"),
                                             vmem_limit_bytes=VMEM_LIMIT),
        name="attn_out",
    )(o, gate, h, p, w_out, norm_post, ple_w, ple_gate, ple_norm)


def _rope_patterns(s):
    half = ROPE_DIMS // 2
    inv = ROPE_THETA ** (-jnp.arange(0, ROPE_DIMS, 2, dtype=F32) / ROPE_DIMS)
    ang = jnp.arange(s, dtype=F32)[:, None] * inv[None, :]
    cos, sin = jnp.cos(ang), jnp.sin(ang)
    lane = jnp.arange(PAIR) % HEAD
    idx = lane % half
    c = jnp.where(lane < ROPE_DIMS, cos[:, idx], 1.0)
    s1 = jnp.where(lane < half, -sin[:, idx], 0.0)
    s2 = jnp.where((lane >= half) & (lane < ROPE_DIMS), sin[:, idx], 0.0)
    return c, s1, s2


def kernel(x, p, norm_pre, norm_post, a_mu, a_w_in, a_w0, a_w1, a_w2, a_a0, a_a1, a_a2, a_k_k, a_k_a,
           a_r_k, a_lnx_g, a_lnx_b, a_w_out, kv_norm, kv_w, b_w_in, b_lambda, b_subln, b_w_out, ple_w,
           ple_gate, ple_norm):
    b, s, d = x.shape
    assert d % PAIR == 0
    t_proj, t_scan, t_q, t_kv, t_rows = _tiles(s)
    row = lambda t: t.reshape(1, d).astype(F32)
    bf = lambda t: t.astype(BF16)

    head_of_lane = jnp.arange(d)[:, None] // HEAD
    seg = (head_of_lane == jnp.arange(PAIR)[None, :]).astype(BF16)
    segt = seg.T
    cos, s1, s2 = _rope_patterns(s)

    r, k, v, g, kk, beta, logw, bonus = _rwkv_in(
        x, row(norm_pre[0]), a_mu[0], bf(a_w_in[0]), row(a_w0[0]), bf(a_w1[0]), bf(a_w2[0]),
        row(a_a0[0]), bf(a_a1[0]), bf(a_a2[0]), row(a_k_k[0]), row(a_k_a[0]), row(a_r_k[0]),
        seg, segt, t_proj)
    y = _rwkv_scan(r, k, v, kk, beta, logw, t_scan)
    h, q, kr, vv, gate = _mid(
        y, g, bonus, x, p, seg, segt, row(a_lnx_g[0]), row(a_lnx_b[0]), bf(a_w_out[0]),
        row(norm_post[0]), bf(ple_w[0]), bf(ple_gate[0]), row(ple_norm[0]), row(kv_norm), bf(kv_w),
        row(norm_pre[1]), bf(b_w_in[0]), cos, s1, s2, t_proj)
    o = _diff_attn(q, kr, vv, b_lambda[0].astype(F32), b_subln[0].reshape(1, PAIR).astype(F32), t_q, t_kv,
                   t_rows)
    return _attn_out(o, gate, h, p, bf(b_w_out[0]), row(norm_post[1]), bf(ple_w[1]), bf(ple_gate[1]),
                     row(ple_norm[1]), t_proj)
```

```python
import math

import jax
import jax.numpy as jnp
from jax import lax
from jax.experimental import pallas as pl
from jax.experimental.pallas import tpu as pltpu

F32 = jnp.float32
BF16 = jnp.bfloat16

HEAD = 64
PAIR = 2 * HEAD
CHUNK = 64
ROPE_DIMS = 16
ROPE_THETA = 500000.0
NORM_EPS = 1e-6
SUBLN_EPS = 1e-5
GN_EPS = 64e-5
LAM_INIT_1 = 0.8 - 0.6 * math.exp(-0.3 * 1)
DECAY_SCALE = math.exp(-0.5)
LOG2E = math.log2(math.e)
MASK_VALUE = -1e30
VMEM_LIMIT = 56 * 1024 * 1024


def _tiles(s):
    proj = min(512, s)
    scan = min(2 * CHUNK, s)
    q = min(1024, s)
    rows = min(256, q)
    assert s % proj == 0 and s % scan == 0 and s % q == 0 and scan % CHUNK == 0
    return proj, scan, q, rows


def _dot(a, b):
    return jnp.dot(a.astype(BF16), b.astype(BF16), preferred_element_type=F32)


def _dot_nt(a, b):
    return lax.dot_general(a.astype(BF16), b.astype(BF16), (((1,), (1,)), ((), ())),
                           preferred_element_type=F32)


def _dot_tn(a, b):
    return lax.dot_general(a.astype(BF16), b.astype(BF16), (((0,), (0,)), ((), ())),
                           preferred_element_type=F32)


def _split(a):
    hi = a.astype(BF16)
    lo = (a - hi.astype(F32)).astype(BF16)
    return hi, lo


def _rms(x, g, eps):
    return x * lax.rsqrt(jnp.mean(x * x, axis=-1, keepdims=True) + eps) * g


def _sigmoid(x):
    return 1.0 / (1.0 + jnp.exp(-x))


def _rope(x, c, s1, s2):
    d = x.shape[-1]
    rep = d // PAIR
    c = jnp.tile(c, (1, rep))
    s1 = jnp.tile(s1, (1, rep))
    s2 = jnp.tile(s2, (1, rep))
    up = pltpu.roll(x, d - ROPE_DIMS // 2, 1)
    dn = pltpu.roll(x, ROPE_DIMS // 2, 1)
    return x * c + up * s1 + dn * s2


def _row_groups(t, n=2):
    return [pl.ds(i * (t // n), t // n) for i in range(n)] if t % (8 * n) == 0 else [pl.ds(0, t)]


def _const_spec(shape):
    nd = len(shape)
    return pl.BlockSpec(shape, lambda *_: (0,) * nd)


def _rwkv_in_kernel(x_ref, xp_ref, g_ref, mu_ref, win_ref, w0_ref, w1_ref, w2_ref, a0_ref, a1_ref,
                    a2_ref, kk_ref, ka_ref, rk_ref, seg_ref, segt_ref,
                    r_out, k_out, v_out, g_out, kkn_out, beta_out, logw_out, bonus_out):
    s = pl.program_id(1)
    gain = g_ref[...]
    xn = _rms(x_ref[0], gain, NORM_EPS)
    t = xn.shape[0]
    prev_last = _rms(xp_ref[0], gain, NORM_EPS)[7:8]
    prev_last = jnp.where(s > 0, prev_last, 0.0)
    row = lax.broadcasted_iota(jnp.int32, (t, 1), 0)
    xprev = jnp.where(row == 0, prev_last, pltpu.roll(xn, 1, 0))
    dx = xprev - xn
    mix = lambda c: xn + dx * mu_ref[c:c + 1, :]

    seg = seg_ref[...]
    segt = segt_ref[...]
    w_lora = _dot(mix(4), w1_ref[...])
    a_lora = _dot(mix(5), a1_ref[...])
    k = _dot(mix(1), win_ref[1])
    w_pre = w0_ref[...] + _dot(jnp.tanh(w_lora), w2_ref[...])
    a = _sigmoid(a0_ref[...] + _dot(a_lora, a2_ref[...]))
    r = _dot(mix(0), win_ref[0])
    logw_out[0] = -DECAY_SCALE * _sigmoid(w_pre)
    kk = k * kk_ref[...]
    sumsq = _dot(_dot(kk * kk, seg), segt)
    k = k * (1.0 + (a - 1.0) * ka_ref[...])
    k_out[0] = k.astype(BF16)
    v = _dot(mix(2), win_ref[2])
    kk = kk * lax.rsqrt(jnp.maximum(sumsq, 1e-24))
    kkn_out[0] = kk.astype(BF16)
    beta_out[0] = (kk * a).astype(BF16)
    r_out[0] = r.astype(BF16)
    rk = _dot(_dot(r * k * rk_ref[...], seg), segt)
    g = _dot(mix(3), win_ref[3])
    v_out[0] = v.astype(BF16)
    bonus_out[0] = (rk * v).astype(BF16)
    g_out[0] = g.astype(BF16)


def _rwkv_in(x, norm_pre, mu, w_in, w0, w1, w2, a0, a1, a2, k_k, k_a, r_k, seg, segt, tile):
    b, s, d = x.shape
    lora = w1.shape[1]
    tok = pl.BlockSpec((1, tile, d), lambda i, j: (i, j, 0))
    prev = pl.BlockSpec((1, 8, d), lambda i, j: (i, jnp.maximum(j * (tile // 8) - 1, 0), 0))
    row = _const_spec((1, d))
    out_bf = jax.ShapeDtypeStruct((b, s, d), BF16)
    out_f32 = jax.ShapeDtypeStruct((b, s, d), F32)
    return pl.pallas_call(
        _rwkv_in_kernel,
        grid=(b, s // tile),
        in_specs=[tok, prev, row, _const_spec((6, d)), _const_spec((4, d, d)), row,
                  _const_spec((d, lora)), _const_spec((lora, d)), row, _const_spec((d, lora)),
                  _const_spec((lora, d)), row, row, row, _const_spec((d, PAIR)), _const_spec((PAIR, d))],
        out_specs=[tok] * 8,
        out_shape=[out_bf] * 6 + [out_f32, out_bf],
        compiler_params=pltpu.CompilerParams(dimension_semantics=("parallel", "parallel"),
                                             vmem_limit_bytes=VMEM_LIMIT),
        name="rwkv_in",
    )(x, x, norm_pre, mu, w_in, w0, w1, w2, a0, a1, a2, k_k, k_a, r_k, seg, segt)


def _rwkv_scan_kernel(r_ref, k_ref, v_ref, kk_ref, beta_ref, logw_ref, y_ref, z_ref):
    @pl.when(pl.program_id(1) == 0)
    def _():
        z_ref[...] = jnp.zeros_like(z_ref)

    t, d = r_ref.shape[1], r_ref.shape[2]
    c = CHUNK
    pairs = d // PAIR
    probs = [(ch, h) for ch in range(t // c) for h in range(pairs)]

    ri = lax.broadcasted_iota(jnp.int32, (PAIR, PAIR), 0)
    ci = lax.broadcasted_iota(jnp.int32, (PAIR, PAIR), 1)
    same_head = (ri // HEAD) == (ci // HEAD)
    strict = ri > ci
    eye = ri == ci
    ti = lax.broadcasted_iota(jnp.int32, (c, 2 * PAIR), 0)
    si = lax.broadcasted_iota(jnp.int32, (c, 2 * PAIR), 1)
    incl = (si % HEAD) <= ti
    head0 = lax.broadcasted_iota(jnp.int32, (c, PAIR), 1) < HEAD
    tri = jnp.where(lax.broadcasted_iota(jnp.int32, (c, c), 1)
                    <= lax.broadcasted_iota(jnp.int32, (c, c), 0), 1.0, 0.0).astype(BF16)

    def stack_heads(m):
        return jnp.concatenate([jnp.where(head0, m, 0.0), jnp.where(head0, 0.0, m)], axis=0)

    def load(ref, ch, h):
        return ref[0, pl.ds(ch * c, c), pl.ds(h * PAIR, PAIR)]

    lws = [load(logw_ref, ch, h) for ch, h in probs]
    cums = []
    for lw in lws:
        hi, lo = _split(lw)
        both = _dot(tri, jnp.concatenate([hi, lo], axis=1))
        cums.append(both[:, :PAIR] + both[:, PAIR:])
    ops = []
    for (ch, h), lw, cum in zip(probs, lws, cums):
        r = load(r_ref, ch, h).astype(F32)
        k = load(k_ref, ch, h).astype(F32)
        v = load(v_ref, ch, h).astype(F32)
        kk = load(kk_ref, ch, h).astype(F32)
        beta = load(beta_ref, ch, h).astype(F32)
        cum_end = cum[c - 1:c, :]
        e_neg = jnp.exp(-cum)
        e_end = jnp.exp(cum_end - cum)
        ops.append(dict(
            r_t=r * jnp.exp(cum), xa=stack_heads(-kk * jnp.exp(cum - lw)).astype(BF16),
            bk_h=jnp.concatenate([stack_heads(beta * e_neg), stack_heads(k * e_neg)], axis=0).astype(BF16),
            b_e=(beta * e_end).astype(BF16), k_e=(k * e_end).astype(BF16),
            v=v.astype(BF16), vst=stack_heads(v).astype(BF16), w_end=jnp.exp(cum_end)))

    for o in ops:
        g = _dot_nt(jnp.concatenate([o["xa"], o["r_t"].astype(BF16)], axis=0), o["bk_h"])
        o["a_ab"] = jnp.where(strict, g[:PAIR, :PAIR], 0.0)
        o["a_ak"] = jnp.where(strict, g[:PAIR, PAIR:], 0.0).astype(BF16)
        o["a_q"] = jnp.where(incl, g[PAIR:, :], 0.0).astype(BF16)
    for o in ops:
        o["akv"] = _dot(o["a_ak"], o["vst"])

    for o in ops:
        o["tinv"] = jnp.where(eye, 1.0, 0.0) + jnp.where((ri // 2 == ci // 2), o["a_ab"], 0.0)
    half = 2
    while half < c:
        blk = 2 * half
        sel = (ri // blk == ci // blk) & (ri % blk >= half) & (ci % blk < half)
        for o in ops:
            o["tmp"] = _dot(o["tinv"], jnp.where(sel, o["a_ab"], 0.0))
        for o in ops:
            o["tinv"] = o["tinv"] + _dot(o["tmp"], o["tinv"])
        half = blk

    for o in ops:
        o["wu_st"] = _dot(o["tinv"], jnp.concatenate([o["xa"], o["akv"].astype(BF16)], axis=1))
    for o in ops:
        o["rq_y"] = _dot(o["a_q"][:, :PAIR], o["wu_st"])
        o["y_kv"] = _dot(o["a_q"][:, PAIR:], o["vst"])
    for o in ops:
        wu = o["wu_st"][:c] + o["wu_st"][c:]
        o["pq"] = _dot_tn(o["b_e"], wu)
        o["q_kv"] = _dot_tn(o["k_e"], o["v"])
    for o in ops:
        o["rq"] = o["r_t"] + o["rq_y"][:, :PAIR]
        o["y_in"] = o["rq_y"][:, PAIR:] + o["y_kv"]
        o["p_t"] = jnp.where(eye, o["w_end"], 0.0) + jnp.where(same_head, o["pq"][:, :PAIR], 0.0)
        o["q_t"] = jnp.where(same_head, o["pq"][:, PAIR:] + o["q_kv"], 0.0)

    zs = [z_ref[h] for h in range(pairs)]
    for ch in range(t // c):
        chunk_ops = ops[ch * pairs:(ch + 1) * pairs]
        zbs = [z.astype(BF16) for z in zs]
        for h, (o, zb) in enumerate(zip(chunk_ops, zbs)):
            y = _dot(o["rq"], zb) + o["y_in"]
            y_ref[0, pl.ds(ch * c, c), pl.ds(h * PAIR, PAIR)] = y.astype(y_ref.dtype)
        new = []
        for o, zb in zip(chunk_ops, zbs):
            ph, plo = _split(o["p_t"])
            new.append(_dot(ph, zb) + _dot(plo, zb) + o["q_t"])
        zs = new
    for h in range(pairs):
        z_ref[h] = zs[h]


def _rwkv_scan(r, k, v, kk, beta, logw, tile):
    b, s, d = r.shape
    tok = pl.BlockSpec((1, tile, d), lambda i, j: (i, j, 0))
    return pl.pallas_call(
        _rwkv_scan_kernel,
        grid=(b, s // tile),
        in_specs=[tok] * 6,
        out_specs=tok,
        out_shape=jax.ShapeDtypeStruct((b, s, d), BF16),
        scratch_shapes=[pltpu.VMEM((d // PAIR, PAIR, PAIR), F32)],
        compiler_params=pltpu.CompilerParams(
            dimension_semantics=("parallel", "arbitrary"), vmem_limit_bytes=VMEM_LIMIT),
        name="rwkv_scan",
    )(r, k, v, kk, beta, logw)


def _ple_groups(hs, ps, plew_ref, pleg_ref, plen_ref):
    es = [_dot(p, plew_ref[...]) for p in ps]
    gates = [_dot(h, pleg_ref[...]) for h in hs]
    return [h + _rms(_sigmoid(gate) * e, plen_ref[...], NORM_EPS) for h, gate, e in zip(hs, gates, es)]


def _mid_kernel(y_ref, g_ref, bonus_ref, x_ref, p_ref, seg_ref, segt_ref, lng_ref, lnb_ref, wout_ref,
                npost_ref, plew_ref, pleg_ref, plen_ref, kvn_ref, kvw_ref, npre1_ref, bwin_ref,
                cos_ref, s1_ref, s2_ref,
                h_out, q_out, k_out, v_out, gate_out):
    d = x_ref.shape[-1]
    groups = _row_groups(x_ref.shape[1])
    both = lambda f: [f(rows) for rows in groups]

    ys = both(lambda rows: y_ref[0, rows, :])
    means = [_dot(_dot(y, seg_ref[...]), segt_ref[...]) * (1.0 / HEAD) for y in ys]
    ycs = [y.astype(F32) - mean for y, mean in zip(ys, means)]
    vars_ = [_dot(_dot(yc * yc, seg_ref[...]), segt_ref[...]) * (1.0 / HEAD) for yc in ycs]
    zs = []
    for rows, yc, var in zip(groups, ycs, vars_):
        yn = yc * lax.rsqrt(var + GN_EPS) * lng_ref[...] + lnb_ref[...] + bonus_ref[0, rows, :].astype(F32)
        g = g_ref[0, rows, :].astype(F32)
        zs.append((yn * (g * _sigmoid(g))).astype(BF16))
    mixed = [_dot(z, wout_ref[...]) for z in zs]
    hs = [x_ref[0, rows, :] + _rms(m, npost_ref[...], NORM_EPS) for rows, m in zip(groups, mixed)]
    hs = _ple_groups(hs, both(lambda rows: p_ref[0, 0, rows, :]), plew_ref, pleg_ref, plen_ref)
    for rows, h in zip(groups, hs):
        h_out[0, rows, :] = h

    tabs = both(lambda rows: (cos_ref[rows, :], s1_ref[rows, :], s2_ref[rows, :]))
    kvs = [_dot(_rms(h, kvn_ref[...], NORM_EPS), kvw_ref[...]) for h in hs]
    projs = [_dot(_rms(h, npre1_ref[...], NORM_EPS), bwin_ref[...]) for h in hs]
    for rows, kv, proj, tab in zip(groups, kvs, projs, tabs):
        k_out[0, rows, :] = _rope(kv[:, :d], *tab).astype(BF16)
        v_out[0, rows, :] = kv[:, d:].astype(BF16)
        q_out[0, rows, :] = (_rope(proj[:, :d], *tab) * (HEAD ** -0.5 * LOG2E)).astype(BF16)
        gate_out[0, rows, :] = proj[:, d:].astype(BF16)


def _mid(y, g, bonus, x, p, seg, segt, lnx_g, lnx_b, w_out, norm_post, ple_w, ple_gate, ple_norm,
         kv_norm, kv_w, norm_pre1, b_w_in, cos, s1, s2, tile):
    b, s, d = x.shape
    pd = p.shape[-1]
    tok = pl.BlockSpec((1, tile, d), lambda i, j: (i, j, 0))
    ptok = pl.BlockSpec((1, 1, tile, pd), lambda i, j: (0, i, j, 0))
    tab = pl.BlockSpec((tile, PAIR), lambda i, j: (j, 0))
    row = _const_spec((1, d))
    out_bf = jax.ShapeDtypeStruct((b, s, d), BF16)
    return pl.pallas_call(
        _mid_kernel,
        grid=(b, s // tile),
        in_specs=[tok, tok, tok, tok, ptok, _const_spec((d, PAIR)), _const_spec((PAIR, d)), row, row,
                  _const_spec((d, d)), row, _const_spec((pd, d)), _const_spec((d, d)), row, row,
                  _const_spec((d, 2 * d)), row, _const_spec((d, 2 * d)), tab, tab, tab],
        out_specs=[tok] * 5,
        out_shape=[jax.ShapeDtypeStruct((b, s, d), F32)] + [out_bf] * 4,
        compiler_params=pltpu.CompilerParams(dimension_semantics=("parallel", "parallel"),
                                             vmem_limit_bytes=VMEM_LIMIT),
        name="mid",
    )(y, g, bonus, x, p, seg, segt, lnx_g, lnx_b, w_out, norm_post, ple_w, ple_gate, ple_norm,
      kv_norm, kv_w, norm_pre1, b_w_in, cos, s1, s2)


def _make_diff_attn_kernel(tq, rows):
    assert tq % rows == 0

    def kernel(q_ref, k_ref, v_ref, lam_ref, subln_ref, o_ref, qs_ref, m_ref, acc_ref):
        i = pl.program_id(2)
        q = q_ref[0]
        lane = lax.broadcasted_iota(jnp.int32, q.shape, 1)
        zero = jnp.zeros_like(q)
        qs_ref[...] = jnp.concatenate([jnp.where(lane < HEAD, q, zero), jnp.where(lane < HEAD, zero, q)],
                                      axis=0)
        groups = [pl.ds(g * rows, rows) for g in range(2 * tq // rows)]

        def step(start, width, diagonal):
            keys = pl.ds(pl.multiple_of(start, tq), width)
            k_blk = k_ref[0, keys, :]
            v_ext = jnp.concatenate([v_ref[0, keys, :], jnp.ones((width, PAIR), BF16)], axis=1)
            widths = [((n * rows) % tq + rows if diagonal else width) for n in range(len(groups))]
            ss = [_dot_nt(qs_ref[g, :], k_blk[:w]) for g, w in zip(groups, widths)]
            if diagonal:
                masks = {}
                for n, w in enumerate(widths):
                    off = (n * rows) % tq
                    if off not in masks:
                        masks[off] = (lax.broadcasted_iota(jnp.int32, (rows, w), 1)
                                      <= lax.broadcasted_iota(jnp.int32, (rows, w), 0) + off)
                    ss[n] = jnp.where(masks[off], ss[n], MASK_VALUE)
            if diagonal:
                m_news = [jnp.broadcast_to(jnp.max(s, axis=-1, keepdims=True), (rows, PAIR)) for s in ss]
            else:
                m_olds = [m_ref[g, :] for g in groups]
                m_news = [jnp.maximum(m_old, jnp.max(s, axis=-1, keepdims=True))
                          for m_old, s in zip(m_olds, ss)]
            ps = [jnp.exp2(s - jnp.tile(m_new, (1, w // PAIR))).astype(BF16)
                  for s, m_new, w in zip(ss, m_news, widths)]
            pvs = [_dot(p, v_ext[:w]) for p, w in zip(ps, widths)]
            for n, (g, m_new, pv) in enumerate(zip(groups, m_news, pvs)):
                if diagonal:
                    acc_ref[g, :] = pv
                else:
                    scale = jnp.exp2(m_olds[n] - m_new)
                    acc_ref[g, :] = jnp.tile(scale, (1, 2)) * acc_ref[g, :] + pv
                m_ref[g, :] = m_new

        def full_block(j, carry):
            step(j * tq, tq, False)
            return carry

        step(i * tq, tq, True)
        lax.fori_loop(0, i, full_block, 0)

        lq = lam_ref[...]
        lam = (jnp.exp(jnp.sum(lq[0:1] * lq[1:2], axis=-1, keepdims=True))
               - jnp.exp(jnp.sum(lq[2:3] * lq[3:4], axis=-1, keepdims=True)) + LAM_INIT_1)
        acc = acc_ref[...]
        o = acc[:, :PAIR] / acc[:, PAIR:]
        o = o[:tq] - lam * o[tq:]
        o = _rms(o, subln_ref[...], SUBLN_EPS) * (1.0 - LAM_INIT_1)
        o_ref[0] = o.astype(o_ref.dtype)

    return kernel


def _diff_attn(q, k, v, lam_qk, subln, tq, rows):
    b, s, d = q.shape
    qtok = pl.BlockSpec((1, tq, PAIR), lambda i, h, j: (i, j, h))
    seq = pl.BlockSpec((1, s, PAIR), lambda i, h, j: (i, 0, h))
    return pl.pallas_call(
        _make_diff_attn_kernel(tq, rows),
        grid=(b, d // PAIR, s // tq),
        in_specs=[qtok, seq, seq, _const_spec(lam_qk.shape), _const_spec((1, PAIR))],
        out_specs=qtok,
        out_shape=jax.ShapeDtypeStruct((b, s, d), BF16),
        scratch_shapes=[pltpu.VMEM((2 * tq, PAIR), BF16), pltpu.VMEM((2 * tq, PAIR), F32),
                        pltpu.VMEM((2 * tq, 2 * PAIR), F32)],
        compiler_params=pltpu.CompilerParams(
            dimension_semantics=("parallel", "parallel", "arbitrary"), vmem_limit_bytes=VMEM_LIMIT),
        name="diff_attn",
    )(q, k, v, lam_qk, subln)


def _attn_out_kernel(o_ref, gate_ref, h_ref, p_ref, wout_ref, npost_ref, plew_ref, pleg_ref, plen_ref,
                     out_ref):
    groups = _row_groups(o_ref.shape[1])

    def gate_and_project(rows):
        g = gate_ref[0, rows, :].astype(F32)
        return _dot(o_ref[0, rows, :].astype(F32) * (g * _sigmoid(g)), wout_ref[...])

    def residual_and_ple_matmuls(rows, mixed):
        h = h_ref[0, rows, :] + _rms(mixed, npost_ref[...], NORM_EPS)
        return h, _dot(p_ref[0, 0, rows, :], plew_ref[...]), _dot(h, pleg_ref[...])

    def finish(rows, h, e, gate):
        out_ref[0, rows, :] = h + _rms(_sigmoid(gate) * e, plen_ref[...], NORM_EPS)

    mixed = [gate_and_project(rows) for rows in groups]
    mids = [residual_and_ple_matmuls(rows, m) for rows, m in zip(groups, mixed)]
    for rows, (h, e, gate) in zip(groups, mids):
        finish(rows, h, e, gate)


def _attn_out(o, gate, h, p, w_out, norm_post, ple_w, ple_gate, ple_norm, tile):
    b, s, d = h.shape
    pd = p.shape[-1]
    tok = pl.BlockSpec((1, tile, d), lambda i, j: (i, j, 0))
    ptok = pl.BlockSpec((1, 1, tile, pd), lambda i, j: (1, i, j, 0))
    row = _const_spec((1, d))
    return pl.pallas_call(
        _attn_out_kernel,
        grid=(b, s // tile),
        in_specs=[tok, tok, tok, ptok, _const_spec((d, d)), row, _const_spec((pd, d)),
                  _const_spec((d, d)), row],
        out_specs=tok,
        out_shape=jax.ShapeDtypeStruct((b, s, d), F32),
        compiler_params=pltpu.CompilerParams(dimension_semantics=("parallel", "parallel"),
                                             vmem_limit_bytes=VMEM_LIMIT),
        name="attn_out",
    )(o, gate, h, p, w_out, norm_post, ple_w, ple_gate, ple_norm)


def _rope_patterns(s):
    half = ROPE_DIMS // 2
    inv = ROPE_THETA ** (-jnp.arange(0, ROPE_DIMS, 2, dtype=F32) / ROPE_DIMS)
    ang = jnp.arange(s, dtype=F32)[:, None] * inv[None, :]
    cos, sin = jnp.cos(ang), jnp.sin(ang)
    lane = jnp.arange(PAIR) % HEAD
    idx = lane % half
    c = jnp.where(lane < ROPE_DIMS, cos[:, idx], 1.0)
    s1 = jnp.where(lane < half, -sin[:, idx], 0.0)
    s2 = jnp.where((lane >= half) & (lane < ROPE_DIMS), sin[:, idx], 0.0)
    return c, s1, s2


def kernel(x, p, norm_pre, norm_post, a_mu, a_w_in, a_w0, a_w1, a_w2, a_a0, a_a1, a_a2, a_k_k, a_k_a,
           a_r_k, a_lnx_g, a_lnx_b, a_w_out, kv_norm, kv_w, b_w_in, b_lambda, b_subln, b_w_out, ple_w,
           ple_gate, ple_norm):
    b, s, d = x.shape
    assert d % PAIR == 0
    t_proj, t_scan, t_q, t_rows = _tiles(s)
    row = lambda t: t.reshape(1, d).astype(F32)
    bf = lambda t: t.astype(BF16)

    head_of_lane = jnp.arange(d)[:, None] // HEAD
    seg = (head_of_lane == jnp.arange(PAIR)[None, :]).astype(BF16)
    segt = seg.T
    cos, s1, s2 = _rope_patterns(s)

    r, k, v, g, kk, beta, logw, bonus = _rwkv_in(
        x, row(norm_pre[0]), a_mu[0], bf(a_w_in[0]), row(a_w0[0]), bf(a_w1[0]), bf(a_w2[0]),
        row(a_a0[0]), bf(a_a1[0]), bf(a_a2[0]), row(a_k_k[0]), row(a_k_a[0]), row(a_r_k[0]),
        seg, segt, t_proj)
    y = _rwkv_scan(r, k, v, kk, beta, logw, t_scan)
    h, q, kr, vv, gate = _mid(
        y, g, bonus, x, p, seg, segt, row(a_lnx_g[0]), row(a_lnx_b[0]), bf(a_w_out[0]),
        row(norm_post[0]), bf(ple_w[0]), bf(ple_gate[0]), row(ple_norm[0]), row(kv_norm), bf(kv_w),
        row(norm_pre[1]), bf(b_w_in[0]), cos, s1, s2, t_proj)
    o = _diff_attn(q, kr, vv, b_lambda[0].astype(F32), b_subln[0].reshape(1, PAIR).astype(F32), t_q, t_rows)
    return _attn_out(o, gate, h, p, bf(b_w_out[0]), row(norm_post[1]), bf(ple_w[1]), bf(ple_gate[1]),
                     row(ple_norm[1]), t_proj)
```

```python
import math

import jax
import jax.numpy as jnp
from jax import lax
from jax.experimental import pallas as pl
from jax.experimental.pallas import tpu as pltpu

F32 = jnp.float32
BF16 = jnp.bfloat16

HEAD = 64
PAIR = 2 * HEAD
CHUNK = 64
ROPE_DIMS = 16
ROPE_THETA = 500000.0
NORM_EPS = 1e-6
SUBLN_EPS = 1e-5
GN_EPS = 64e-5
LAM_INIT_1 = 0.8 - 0.6 * math.exp(-0.3 * 1)
DECAY_SCALE = math.exp(-0.5)
LOG2E = math.log2(math.e)
MASK_VALUE = -1e30
VMEM_LIMIT = 56 * 1024 * 1024


def _tiles(s):
    proj = min(512, s)
    scan = min(4 * CHUNK, s)
    q = min(1024, s)
    rows = min(256, q)
    assert s % proj == 0 and s % scan == 0 and s % q == 0 and scan % CHUNK == 0
    return proj, scan, q, rows


def _dot(a, b):
    return jnp.dot(a.astype(BF16), b.astype(BF16), preferred_element_type=F32)


def _dot_nt(a, b):
    return lax.dot_general(a.astype(BF16), b.astype(BF16), (((1,), (1,)), ((), ())),
                           preferred_element_type=F32)


def _dot_tn(a, b):
    return lax.dot_general(a.astype(BF16), b.astype(BF16), (((0,), (0,)), ((), ())),
                           preferred_element_type=F32)


def _split(a):
    hi = a.astype(BF16)
    lo = (a - hi.astype(F32)).astype(BF16)
    return hi, lo


def _rms(x, g, eps):
    return x * lax.rsqrt(jnp.mean(x * x, axis=-1, keepdims=True) + eps) * g


def _sigmoid(x):
    return 1.0 / (1.0 + jnp.exp(-x))


def _rope(x, c, s1, s2):
    d = x.shape[-1]
    rep = d // PAIR
    c = jnp.tile(c, (1, rep))
    s1 = jnp.tile(s1, (1, rep))
    s2 = jnp.tile(s2, (1, rep))
    up = pltpu.roll(x, d - ROPE_DIMS // 2, 1)
    dn = pltpu.roll(x, ROPE_DIMS // 2, 1)
    return x * c + up * s1 + dn * s2


def _row_groups(t, n=2):
    return [pl.ds(i * (t // n), t // n) for i in range(n)] if t % (8 * n) == 0 else [pl.ds(0, t)]


def _const_spec(shape):
    nd = len(shape)
    return pl.BlockSpec(shape, lambda *_: (0,) * nd)


def _rwkv_in_kernel(x_ref, xp_ref, g_ref, mu_ref, win_ref, w0_ref, w1_ref, w2_ref, a0_ref, a1_ref,
                    a2_ref, kk_ref, ka_ref, rk_ref, seg_ref, segt_ref,
                    r_out, k_out, v_out, g_out, kkn_out, beta_out, logw_out, bonus_out):
    s = pl.program_id(1)
    gain = g_ref[...]
    xn = _rms(x_ref[0], gain, NORM_EPS)
    t = xn.shape[0]
    prev_last = _rms(xp_ref[0], gain, NORM_EPS)[7:8]
    prev_last = jnp.where(s > 0, prev_last, 0.0)
    row = lax.broadcasted_iota(jnp.int32, (t, 1), 0)
    xprev = jnp.where(row == 0, prev_last, pltpu.roll(xn, 1, 0))
    dx = xprev - xn
    mix = lambda c: xn + dx * mu_ref[c:c + 1, :]

    seg = seg_ref[...]
    segt = segt_ref[...]
    w_lora = _dot(mix(4), w1_ref[...])
    a_lora = _dot(mix(5), a1_ref[...])
    k = _dot(mix(1), win_ref[1])
    w_pre = w0_ref[...] + _dot(jnp.tanh(w_lora), w2_ref[...])
    a = _sigmoid(a0_ref[...] + _dot(a_lora, a2_ref[...]))
    r = _dot(mix(0), win_ref[0])
    logw_out[0] = -DECAY_SCALE * _sigmoid(w_pre)
    kk = k * kk_ref[...]
    sumsq = _dot(_dot(kk * kk, seg), segt)
    k = k * (1.0 + (a - 1.0) * ka_ref[...])
    k_out[0] = k.astype(BF16)
    v = _dot(mix(2), win_ref[2])
    kk = kk * lax.rsqrt(jnp.maximum(sumsq, 1e-24))
    kkn_out[0] = kk.astype(BF16)
    beta_out[0] = (kk * a).astype(BF16)
    r_out[0] = r.astype(BF16)
    rk = _dot(_dot(r * k * rk_ref[...], seg), segt)
    g = _dot(mix(3), win_ref[3])
    v_out[0] = v.astype(BF16)
    bonus_out[0] = (rk * v).astype(BF16)
    g_out[0] = g.astype(BF16)


def _rwkv_in(x, norm_pre, mu, w_in, w0, w1, w2, a0, a1, a2, k_k, k_a, r_k, seg, segt, tile):
    b, s, d = x.shape
    lora = w1.shape[1]
    tok = pl.BlockSpec((1, tile, d), lambda i, j: (i, j, 0))
    prev = pl.BlockSpec((1, 8, d), lambda i, j: (i, jnp.maximum(j * (tile // 8) - 1, 0), 0))
    row = _const_spec((1, d))
    out_bf = jax.ShapeDtypeStruct((b, s, d), BF16)
    out_f32 = jax.ShapeDtypeStruct((b, s, d), F32)
    return pl.pallas_call(
        _rwkv_in_kernel,
        grid=(b, s // tile),
        in_specs=[tok, prev, row, _const_spec((6, d)), _const_spec((4, d, d)), row,
                  _const_spec((d, lora)), _const_spec((lora, d)), row, _const_spec((d, lora)),
                  _const_spec((lora, d)), row, row, row, _const_spec((d, PAIR)), _const_spec((PAIR, d))],
        out_specs=[tok] * 8,
        out_shape=[out_bf] * 6 + [out_f32, out_bf],
        compiler_params=pltpu.CompilerParams(dimension_semantics=("parallel", "parallel"),
                                             vmem_limit_bytes=VMEM_LIMIT),
        name="rwkv_in",
    )(x, x, norm_pre, mu, w_in, w0, w1, w2, a0, a1, a2, k_k, k_a, r_k, seg, segt)


def _rwkv_scan_kernel(r_ref, k_ref, v_ref, kk_ref, beta_ref, logw_ref, y_ref, z_ref):
    @pl.when(pl.program_id(1) == 0)
    def _():
        z_ref[...] = jnp.zeros_like(z_ref)

    t, d = r_ref.shape[1], r_ref.shape[2]
    c = CHUNK
    pairs = d // PAIR
    probs = [(ch, h) for ch in range(t // c) for h in range(pairs)]

    ri = lax.broadcasted_iota(jnp.int32, (PAIR, PAIR), 0)
    ci = lax.broadcasted_iota(jnp.int32, (PAIR, PAIR), 1)
    same_head = (ri // HEAD) == (ci // HEAD)
    strict = ri > ci
    eye = ri == ci
    ti = lax.broadcasted_iota(jnp.int32, (c, 2 * PAIR), 0)
    si = lax.broadcasted_iota(jnp.int32, (c, 2 * PAIR), 1)
    incl = (si % HEAD) <= ti
    head0 = lax.broadcasted_iota(jnp.int32, (c, PAIR), 1) < HEAD
    tri = jnp.where(lax.broadcasted_iota(jnp.int32, (c, c), 1)
                    <= lax.broadcasted_iota(jnp.int32, (c, c), 0), 1.0, 0.0).astype(BF16)

    def stack_heads(m):
        return jnp.concatenate([jnp.where(head0, m, 0.0), jnp.where(head0, 0.0, m)], axis=0)

    def load(ref, ch, h):
        return ref[0, pl.ds(ch * c, c), pl.ds(h * PAIR, PAIR)]

    lws = [load(logw_ref, ch, h) for ch, h in probs]
    cums = []
    for lw in lws:
        hi, lo = _split(lw)
        both = _dot(tri, jnp.concatenate([hi, lo], axis=1))
        cums.append(both[:, :PAIR] + both[:, PAIR:])
    ops = []
    for (ch, h), lw, cum in zip(probs, lws, cums):
        r = load(r_ref, ch, h).astype(F32)
        k = load(k_ref, ch, h).astype(F32)
        v = load(v_ref, ch, h).astype(F32)
        kk = load(kk_ref, ch, h).astype(F32)
        beta = load(beta_ref, ch, h).astype(F32)
        cum_end = cum[c - 1:c, :]
        e_neg = jnp.exp(-cum)
        e_end = jnp.exp(cum_end - cum)
        ops.append(dict(
            r_t=r * jnp.exp(cum), xa=stack_heads(-kk * jnp.exp(cum - lw)).astype(BF16),
            bk_h=jnp.concatenate([stack_heads(beta * e_neg), stack_heads(k * e_neg)], axis=0).astype(BF16),
            b_e=(beta * e_end).astype(BF16), k_e=(k * e_end).astype(BF16),
            v=v.astype(BF16), vst=stack_heads(v).astype(BF16), w_end=jnp.exp(cum_end)))

    for o in ops:
        g = _dot_nt(jnp.concatenate([o["xa"], o["r_t"].astype(BF16)], axis=0), o["bk_h"])
        o["a_ab"] = jnp.where(strict, g[:PAIR, :PAIR], 0.0)
        o["a_ak"] = jnp.where(strict, g[:PAIR, PAIR:], 0.0).astype(BF16)
        o["a_q"] = jnp.where(incl, g[PAIR:, :], 0.0).astype(BF16)
    for o in ops:
        o["akv"] = _dot(o["a_ak"], o["vst"])

    for o in ops:
        o["tinv"] = jnp.where(eye, 1.0, 0.0) + jnp.where((ri // 2 == ci // 2), o["a_ab"], 0.0)
    half = 2
    while half < c:
        blk = 2 * half
        sel = (ri // blk == ci // blk) & (ri % blk >= half) & (ci % blk < half)
        for o in ops:
            o["tmp"] = _dot(o["tinv"], jnp.where(sel, o["a_ab"], 0.0))
        for o in ops:
            o["tinv"] = o["tinv"] + _dot(o["tmp"], o["tinv"])
        half = blk

    for o in ops:
        o["wu_st"] = _dot(o["tinv"], jnp.concatenate([o["xa"], o["akv"].astype(BF16)], axis=1))
    zeros_st = jnp.zeros((PAIR, PAIR), BF16)
    zeros_c = jnp.zeros((c, PAIR), BF16)
    for o in ops:
        rhs = jnp.concatenate([o["wu_st"].astype(BF16), jnp.concatenate([zeros_st, o["vst"]], axis=1)], axis=0)
        o["rq_y"] = _dot(o["a_q"], rhs)
    for o in ops:
        wu = (o["wu_st"][:c] + o["wu_st"][c:]).astype(BF16)
        rhs = jnp.concatenate([wu, jnp.concatenate([zeros_c, o["v"]], axis=1)], axis=0)
        o["pq"] = _dot_tn(jnp.concatenate([o["b_e"], o["k_e"]], axis=0), rhs)
    for o in ops:
        rq = o["r_t"] + o["rq_y"][:, :PAIR]
        p_t = jnp.where(eye, o["w_end"], 0.0) + jnp.where(same_head, o["pq"][:, :PAIR], 0.0)
        ph, plo = _split(p_t)
        o["lhs"] = jnp.concatenate([rq.astype(BF16), ph, plo], axis=0)
        o["y_in"] = o["rq_y"][:, PAIR:]
        o["q_t"] = jnp.where(same_head, o["pq"][:, PAIR:], 0.0)

    zs = [z_ref[h] for h in range(pairs)]
    for ch in range(t // c):
        chunk_ops = ops[ch * pairs:(ch + 1) * pairs]
        outs = [_dot(o["lhs"], z) for o, z in zip(chunk_ops, zs)]
        for h, (o, out) in enumerate(zip(chunk_ops, outs)):
            y_ref[0, pl.ds(ch * c, c), pl.ds(h * PAIR, PAIR)] = (out[:c] + o["y_in"]).astype(y_ref.dtype)
        zs = [out[c:c + PAIR] + out[c + PAIR:] + o["q_t"] for o, out in zip(chunk_ops, outs)]
    for h in range(pairs):
        z_ref[h] = zs[h]


def _rwkv_scan(r, k, v, kk, beta, logw, tile):
    b, s, d = r.shape
    tok = pl.BlockSpec((1, tile, d), lambda i, j: (i, j, 0))
    return pl.pallas_call(
        _rwkv_scan_kernel,
        grid=(b, s // tile),
        in_specs=[tok] * 6,
        out_specs=tok,
        out_shape=jax.ShapeDtypeStruct((b, s, d), BF16),
        scratch_shapes=[pltpu.VMEM((d // PAIR, PAIR, PAIR), F32)],
        compiler_params=pltpu.CompilerParams(
            dimension_semantics=("parallel", "arbitrary"), vmem_limit_bytes=VMEM_LIMIT),
        name="rwkv_scan",
    )(r, k, v, kk, beta, logw)


def _ple_groups(hs, ps, plew_ref, pleg_ref, plen_ref):
    es = [_dot(p, plew_ref[...]) for p in ps]
    gates = [_dot(h, pleg_ref[...]) for h in hs]
    return [h + _rms(_sigmoid(gate) * e, plen_ref[...], NORM_EPS) for h, gate, e in zip(hs, gates, es)]


def _mid_kernel(y_ref, g_ref, bonus_ref, x_ref, p_ref, seg_ref, segt_ref, lng_ref, lnb_ref, wout_ref,
                npost_ref, plew_ref, pleg_ref, plen_ref, kvn_ref, kvw_ref, npre1_ref, bwin_ref,
                cos_ref, s1_ref, s2_ref,
                h_out, q_out, k_out, v_out, gate_out):
    d = x_ref.shape[-1]
    groups = _row_groups(x_ref.shape[1])
    both = lambda f: [f(rows) for rows in groups]

    ys = both(lambda rows: y_ref[0, rows, :])
    means = [_dot(_dot(y, seg_ref[...]), segt_ref[...]) * (1.0 / HEAD) for y in ys]
    ycs = [y.astype(F32) - mean for y, mean in zip(ys, means)]
    vars_ = [_dot(_dot(yc * yc, seg_ref[...]), segt_ref[...]) * (1.0 / HEAD) for yc in ycs]
    zs = []
    for rows, yc, var in zip(groups, ycs, vars_):
        yn = yc * lax.rsqrt(var + GN_EPS) * lng_ref[...] + lnb_ref[...] + bonus_ref[0, rows, :].astype(F32)
        g = g_ref[0, rows, :].astype(F32)
        zs.append((yn * (g * _sigmoid(g))).astype(BF16))
    mixed = [_dot(z, wout_ref[...]) for z in zs]
    hs = [x_ref[0, rows, :] + _rms(m, npost_ref[...], NORM_EPS) for rows, m in zip(groups, mixed)]
    hs = _ple_groups(hs, both(lambda rows: p_ref[0, 0, rows, :]), plew_ref, pleg_ref, plen_ref)
    for rows, h in zip(groups, hs):
        h_out[0, rows, :] = h

    tabs = both(lambda rows: (cos_ref[rows, :], s1_ref[rows, :], s2_ref[rows, :]))
    kvs = [_dot(_rms(h, kvn_ref[...], NORM_EPS), kvw_ref[...]) for h in hs]
    projs = [_dot(_rms(h, npre1_ref[...], NORM_EPS), bwin_ref[...]) for h in hs]
    for rows, kv, proj, tab in zip(groups, kvs, projs, tabs):
        k_out[0, rows, :] = _rope(kv[:, :d], *tab).astype(BF16)
        v_out[0, rows, :] = kv[:, d:].astype(BF16)
        q_out[0, rows, :] = (_rope(proj[:, :d], *tab) * (HEAD ** -0.5 * LOG2E)).astype(BF16)
        gate_out[0, rows, :] = proj[:, d:].astype(BF16)


def _mid(y, g, bonus, x, p, seg, segt, lnx_g, lnx_b, w_out, norm_post, ple_w, ple_gate, ple_norm,
         kv_norm, kv_w, norm_pre1, b_w_in, cos, s1, s2, tile):
    b, s, d = x.shape
    pd = p.shape[-1]
    tok = pl.BlockSpec((1, tile, d), lambda i, j: (i, j, 0))
    ptok = pl.BlockSpec((1, 1, tile, pd), lambda i, j: (0, i, j, 0))
    tab = pl.BlockSpec((tile, PAIR), lambda i, j: (j, 0))
    row = _const_spec((1, d))
    out_bf = jax.ShapeDtypeStruct((b, s, d), BF16)
    return pl.pallas_call(
        _mid_kernel,
        grid=(b, s // tile),
        in_specs=[tok, tok, tok, tok, ptok, _const_spec((d, PAIR)), _const_spec((PAIR, d)), row, row,
                  _const_spec((d, d)), row, _const_spec((pd, d)), _const_spec((d, d)), row, row,
                  _const_spec((d, 2 * d)), row, _const_spec((d, 2 * d)), tab, tab, tab],
        out_specs=[tok] * 5,
        out_shape=[jax.ShapeDtypeStruct((b, s, d), F32)] + [out_bf] * 4,
        compiler_params=pltpu.CompilerParams(dimension_semantics=("parallel", "parallel"),
                                             vmem_limit_bytes=VMEM_LIMIT),
        name="mid",
    )(y, g, bonus, x, p, seg, segt, lnx_g, lnx_b, w_out, norm_post, ple_w, ple_gate, ple_norm,
      kv_norm, kv_w, norm_pre1, b_w_in, cos, s1, s2)


def _make_diff_attn_kernel(tq, rows):
    assert tq % rows == 0

    def kernel(q_ref, k_ref, v_ref, lam_ref, subln_ref, o_ref, qs_ref, m_ref, acc_ref):
        i = pl.program_id(2)
        q = q_ref[0]
        lane = lax.broadcasted_iota(jnp.int32, q.shape, 1)
        zero = jnp.zeros_like(q)
        qs_ref[...] = jnp.concatenate([jnp.where(lane < HEAD, q, zero), jnp.where(lane < HEAD, zero, q)],
                                      axis=0)
        groups = [pl.ds(g * rows, rows) for g in range(2 * tq // rows)]

        def step(start, width, diagonal):
            keys = pl.ds(pl.multiple_of(start, tq), width)
            k_blk = k_ref[0, keys, :]
            v_ext = jnp.concatenate([v_ref[0, keys, :], jnp.ones((width, PAIR), BF16)], axis=1)
            widths = [((n * rows) % tq + rows if diagonal else width) for n in range(len(groups))]
            ss = [_dot_nt(qs_ref[g, :], k_blk[:w]) for g, w in zip(groups, widths)]
            if diagonal:
                masks = {}
                for n, w in enumerate(widths):
                    off = (n * rows) % tq
                    if off not in masks:
                        masks[off] = (lax.broadcasted_iota(jnp.int32, (rows, w), 1)
                                      <= lax.broadcasted_iota(jnp.int32, (rows, w), 0) + off)
                    ss[n] = jnp.where(masks[off], ss[n], MASK_VALUE)
            if diagonal:
                m_news = [jnp.broadcast_to(jnp.max(s, axis=-1, keepdims=True), (rows, PAIR)) for s in ss]
            else:
                m_olds = [m_ref[g, :] for g in groups]
                m_news = [jnp.maximum(m_old, jnp.max(s, axis=-1, keepdims=True))
                          for m_old, s in zip(m_olds, ss)]
            ps = [jnp.exp2(s - jnp.tile(m_new, (1, w // PAIR))).astype(BF16)
                  for s, m_new, w in zip(ss, m_news, widths)]
            pvs = [_dot(p, v_ext[:w]) for p, w in zip(ps, widths)]
            for n, (g, m_new, pv) in enumerate(zip(groups, m_news, pvs)):
                if diagonal:
                    acc_ref[g, :] = pv
                else:
                    scale = jnp.exp2(m_olds[n] - m_new)
                    acc_ref[g, :] = jnp.tile(scale, (1, 2)) * acc_ref[g, :] + pv
                m_ref[g, :] = m_new

        def full_block(j, carry):
            step(j * tq, tq, False)
            return carry

        step(i * tq, tq, True)
        lax.fori_loop(0, i, full_block, 0)

        lq = lam_ref[...]
        lam = (jnp.exp(jnp.sum(lq[0:1] * lq[1:2], axis=-1, keepdims=True))
               - jnp.exp(jnp.sum(lq[2:3] * lq[3:4], axis=-1, keepdims=True)) + LAM_INIT_1)
        acc = acc_ref[...]
        o = acc[:, :PAIR] / acc[:, PAIR:]
        o = o[:tq] - lam * o[tq:]
        o = _rms(o, subln_ref[...], SUBLN_EPS) * (1.0 - LAM_INIT_1)
        o_ref[0] = o.astype(o_ref.dtype)

    return kernel


def _diff_attn(q, k, v, lam_qk, subln, tq, rows):
    b, s, d = q.shape
    qtok = pl.BlockSpec((1, tq, PAIR), lambda i, h, j: (i, j, h))
    seq = pl.BlockSpec((1, s, PAIR), lambda i, h, j: (i, 0, h))
    return pl.pallas_call(
        _make_diff_attn_kernel(tq, rows),
        grid=(b, d // PAIR, s // tq),
        in_specs=[qtok, seq, seq, _const_spec(lam_qk.shape), _const_spec((1, PAIR))],
        out_specs=qtok,
        out_shape=jax.ShapeDtypeStruct((b, s, d), BF16),
        scratch_shapes=[pltpu.VMEM((2 * tq, PAIR), BF16), pltpu.VMEM((2 * tq, PAIR), F32),
                        pltpu.VMEM((2 * tq, 2 * PAIR), F32)],
        compiler_params=pltpu.CompilerParams(
            dimension_semantics=("parallel", "parallel", "arbitrary"), vmem_limit_bytes=VMEM_LIMIT),
        name="diff_attn",
    )(q, k, v, lam_qk, subln)


def _attn_out_kernel(o_ref, gate_ref, h_ref, p_ref, wout_ref, npost_ref, plew_ref, pleg_ref, plen_ref,
                     out_ref):
    groups = _row_groups(o_ref.shape[1])

    def gate_and_project(rows):
        g = gate_ref[0, rows, :].astype(F32)
        return _dot(o_ref[0, rows, :].astype(F32) * (g * _sigmoid(g)), wout_ref[...])

    def residual_and_ple_matmuls(rows, mixed):
        h = h_ref[0, rows, :] + _rms(mixed, npost_ref[...], NORM_EPS)
        return h, _dot(p_ref[0, 0, rows, :], plew_ref[...]), _dot(h, pleg_ref[...])

    def finish(rows, h, e, gate):
        out_ref[0, rows, :] = h + _rms(_sigmoid(gate) * e, plen_ref[...], NORM_EPS)

    mixed = [gate_and_project(rows) for rows in groups]
    mids = [residual_and_ple_matmuls(rows, m) for rows, m in zip(groups, mixed)]
    for rows, (h, e, gate) in zip(groups, mids):
        finish(rows, h, e, gate)


def _attn_out(o, gate, h, p, w_out, norm_post, ple_w, ple_gate, ple_norm, tile):
    b, s, d = h.shape
    pd = p.shape[-1]
    tok = pl.BlockSpec((1, tile, d), lambda i, j: (i, j, 0))
    ptok = pl.BlockSpec((1, 1, tile, pd), lambda i, j: (1, i, j, 0))
    row = _const_spec((1, d))
    return pl.pallas_call(
        _attn_out_kernel,
        grid=(b, s // tile),
        in_specs=[tok, tok, tok, ptok, _const_spec((d, d)), row, _const_spec((pd, d)),
                  _const_spec((d, d)), row],
        out_specs=tok,
        out_shape=jax.ShapeDtypeStruct((b, s, d), F32),
        compiler_params=pltpu.CompilerParams(dimension_semantics=("parallel", "parallel"),
                                             vmem_limit_bytes=VMEM_LIMIT),
        name="attn_out",
    )(o, gate, h, p, w_out, norm_post, ple_w, ple_gate, ple_norm)


def _rope_patterns(s):
    half = ROPE_DIMS // 2
    inv = ROPE_THETA ** (-jnp.arange(0, ROPE_DIMS, 2, dtype=F32) / ROPE_DIMS)
    ang = jnp.arange(s, dtype=F32)[:, None] * inv[None, :]
    cos, sin = jnp.cos(ang), jnp.sin(ang)
    lane = jnp.arange(PAIR) % HEAD
    idx = lane % half
    c = jnp.where(lane < ROPE_DIMS, cos[:, idx], 1.0)
    s1 = jnp.where(lane < half, -sin[:, idx], 0.0)
    s2 = jnp.where((lane >= half) & (lane < ROPE_DIMS), sin[:, idx], 0.0)
    return c, s1, s2


def kernel(x, p, norm_pre, norm_post, a_mu, a_w_in, a_w0, a_w1, a_w2, a_a0, a_a1, a_a2, a_k_k, a_k_a,
           a_r_k, a_lnx_g, a_lnx_b, a_w_out, kv_norm, kv_w, b_w_in, b_lambda, b_subln, b_w_out, ple_w,
           ple_gate, ple_norm):
    b, s, d = x.shape
    assert d % PAIR == 0
    t_proj, t_scan, t_q, t_rows = _tiles(s)
    row = lambda t: t.reshape(1, d).astype(F32)
    bf = lambda t: t.astype(BF16)

    head_of_lane = jnp.arange(d)[:, None] // HEAD
    seg = (head_of_lane == jnp.arange(PAIR)[None, :]).astype(BF16)
    segt = seg.T
    cos, s1, s2 = _rope_patterns(s)

    r, k, v, g, kk, beta, logw, bonus = _rwkv_in(
        x, row(norm_pre[0]), a_mu[0], bf(a_w_in[0]), row(a_w0[0]), bf(a_w1[0]), bf(a_w2[0]),
        row(a_a0[0]), bf(a_a1[0]), bf(a_a2[0]), row(a_k_k[0]), row(a_k_a[0]), row(a_r_k[0]),
        seg, segt, t_proj)
    y = _rwkv_scan(r, k, v, kk, beta, logw, t_scan)
    h, q, kr, vv, gate = _mid(
        y, g, bonus, x, p, seg, segt, row(a_lnx_g[0]), row(a_lnx_b[0]), bf(a_w_out[0]),
        row(norm_post[0]), bf(ple_w[0]), bf(ple_gate[0]), row(ple_norm[0]), row(kv_norm), bf(kv_w),
        row(norm_pre[1]), bf(b_w_in[0]), cos, s1, s2, t_proj)
    o = _diff_attn(q, kr, vv, b_lambda[0].astype(F32), b_subln[0].reshape(1, PAIR).astype(F32), t_q, t_rows)
    return _attn_out(o, gate, h, p, bf(b_w_out[0]), row(norm_post[1]), bf(ple_w[1]), bf(ple_gate[1]),
                     row(ple_norm[1]), t_proj)
```
